```python
import jax, jax.numpy as jnp
from jax import lax
import numpy as np

D_MODEL = 2048
BATCH = 4
SEQ = 4096
DEPTH = 2

N_META = 16
HEAD_DIM = 128
N_HEADS_SB = D_MODEL // (2 * HEAD_DIM)
N_HEADS_FOX = D_MODEL // (2 * HEAD_DIM)
W_SB = N_HEADS_SB * HEAD_DIM
W_FOX = N_HEADS_FOX * HEAD_DIM
W_MIX = W_SB + W_FOX
N_IN = 3 * W_SB + 3 * W_FOX + N_HEADS_FOX
D_FF = 11 * D_MODEL // 4
CONV_WIDTH = 3
Q_BLOCK = 128
EPS = 1e-6

kernel_name = "hymba_stickbreak_fox_convffn"


def rms_norm(x, g):
    xf = x.astype(jnp.float32)
    y = xf * lax.rsqrt(jnp.mean(xf * xf, axis=-1, keepdims=True) + EPS)
    return (y * g.astype(jnp.float32)).astype(x.dtype)


def block_bounds():
    bounds = [(0, N_META)]
    for i in range(SEQ // Q_BLOCK):
        bounds.append((N_META + i * Q_BLOCK, N_META + (i + 1) * Q_BLOCK))
    return bounds


def stick_breaking_attention(q, k, v):
    scale = HEAD_DIM ** -0.5
    outs = []
    for qs, qe in block_bounds():
        z = jnp.einsum('bqhd,bkhd->bhqk', q[:, qs:qe], k[:, :qe]).astype(jnp.float32) * scale
        t_pos = jnp.arange(qs, qe)[:, None]
        s_pos = jnp.arange(qe)[None, :]
        before = s_pos < t_pos
        log_keep = jnp.where(before, -jax.nn.softplus(z), 0.0)
        log_keep_between = lax.cumsum(log_keep, axis=3, reverse=True) - log_keep
        a = jnp.where(before, jnp.exp(jax.nn.log_sigmoid(z) + log_keep_between), 0.0)
        outs.append(jnp.einsum('bhqk,bkhd->bqhd', a.astype(v.dtype), v[:, :qe]))
    return jnp.concatenate(outs, axis=1)


def forgetting_attention(q, k, v, log_f):
    scale = HEAD_DIM ** -0.5
    c = jnp.transpose(jnp.cumsum(log_f, axis=1), (0, 2, 1))
    outs = []
    for qs, qe in block_bounds():
        logits = jnp.einsum('bqhd,bkhd->bhqk', q[:, qs:qe], k[:, :qe]).astype(jnp.float32) * scale
        logits = logits + (c[:, :, qs:qe, None] - c[:, :, None, :qe])
        t_pos = jnp.arange(qs, qe)[:, None]
        s_pos = jnp.arange(qe)[None, :]
        logits = jnp.where(s_pos <= t_pos, logits, -jnp.inf)
        p = jax.nn.softmax(logits, axis=-1)
        outs.append(jnp.einsum('bhqk,bkhd->bqhd', p.astype(v.dtype), v[:, :qe]))
    return jnp.concatenate(outs, axis=1)


def causal_depthwise_conv(a, w, bias):
    c = a.shape[-1]
    out = lax.conv_general_dilated(
        a, w.astype(a.dtype)[:, None, :], window_strides=(1,),
        padding=[(CONV_WIDTH - 1, 0)], dimension_numbers=('NWC', 'WIO', 'NWC'),
        feature_group_count=c)
    return out + bias.astype(a.dtype)


def hybrid_layer(h, g_mix_pre, w_in, b_f, g_sb, g_fox, w_out, g_mix_post,
                 g_ffn_pre, w_up, conv_w, conv_b, w_down, g_ffn_post):
    b, l, _ = h.shape
    u = rms_norm(h, g_mix_pre)
    proj = u @ w_in
    splits = [W_SB, 2 * W_SB, 3 * W_SB, 3 * W_SB + W_FOX, 3 * W_SB + 2 * W_FOX, 3 * W_SB + 3 * W_FOX]
    q_sb, k_sb, v_sb, q_fx, k_fx, v_fx, f_logit = jnp.split(proj, splits, axis=-1)
    heads_sb = lambda t: t.reshape(b, l, N_HEADS_SB, HEAD_DIM)
    heads_fx = lambda t: t.reshape(b, l, N_HEADS_FOX, HEAD_DIM)
    o_sb = stick_breaking_attention(heads_sb(q_sb), heads_sb(k_sb), heads_sb(v_sb))
    log_f = jax.nn.log_sigmoid((f_logit + b_f).astype(jnp.float32))
    o_fx = forgetting_attention(heads_fx(q_fx), heads_fx(k_fx), heads_fx(v_fx), log_f)
    o_sb = rms_norm(o_sb, g_sb).reshape(b, l, W_SB)
    o_fx = rms_norm(o_fx, g_fox).reshape(b, l, W_FOX)
    mix = jnp.concatenate([o_sb, o_fx], axis=-1) @ w_out
    h = h + rms_norm(mix, g_mix_post)
    u = rms_norm(h, g_ffn_pre)
    a = causal_depthwise_conv(u @ w_up, conv_w, conv_b)
    gate, up = jnp.split(a, [D_FF], axis=-1)
    ff = (jax.nn.silu(gate) * up) @ w_down
    return h + rms_norm(ff, g_ffn_post)


def setup_inputs(seed: int = 0) -> dict:
    key = jax.random.key(seed)
    ks = jax.random.split(key, 16)
    f32 = jnp.float32
    nrm = lambda k, shape, s: jax.random.normal(k, shape, f32) * s
    gain = lambda k, shape: 1.0 + 0.02 * jax.random.normal(k, shape, f32)
    return {
        "x": nrm(ks[0], (BATCH, SEQ, D_MODEL), 1.0),
        "meta": nrm(ks[1], (N_META, D_MODEL), 1.0),
        "g_mix_pre": gain(ks[2], (DEPTH, D_MODEL)),
        "w_in": nrm(ks[3], (DEPTH, D_MODEL, N_IN), D_MODEL ** -0.5),
        "b_f": 3.0 + 0.5 * jax.random.normal(ks[4], (DEPTH, N_HEADS_FOX), f32),
        "g_sb": gain(ks[5], (DEPTH, N_HEADS_SB, HEAD_DIM)),
        "g_fox": gain(ks[6], (DEPTH, N_HEADS_FOX, HEAD_DIM)),
        "w_out": nrm(ks[7], (DEPTH, W_MIX, D_MODEL), W_MIX ** -0.5),
        "g_mix_post": gain(ks[8], (DEPTH, D_MODEL)),
        "g_ffn_pre": gain(ks[9], (DEPTH, D_MODEL)),
        "w_up": nrm(ks[10], (DEPTH, D_MODEL, 2 * D_FF), D_MODEL ** -0.5),
        "conv_w": nrm(ks[11], (DEPTH, CONV_WIDTH, 2 * D_FF), CONV_WIDTH ** -0.5),
        "conv_b": nrm(ks[12], (DEPTH, 2 * D_FF), 0.01),
        "w_down": nrm(ks[13], (DEPTH, D_FF, D_MODEL), D_FF ** -0.5),
        "g_ffn_post": gain(ks[14], (DEPTH, D_MODEL)),
    }


def reference(x, meta, g_mix_pre, w_in, b_f, g_sb, g_fox, w_out, g_mix_post,
              g_ffn_pre, w_up, conv_w, conv_b, w_down, g_ffn_post):
    b = x.shape[0]
    meta_b = jnp.broadcast_to(meta[None].astype(x.dtype), (b, N_META, D_MODEL))
    h = jnp.concatenate([meta_b, x], axis=1)
    for i in range(DEPTH):
        h = hybrid_layer(h, g_mix_pre[i], w_in[i], b_f[i], g_sb[i], g_fox[i], w_out[i],
                         g_mix_post[i], g_ffn_pre[i], w_up[i], conv_w[i], conv_b[i],
                         w_down[i], g_ffn_post[i])
    return h[:, N_META:]
```

```python
from functools import partial

import jax
import jax.numpy as jnp
from jax import lax
from jax.experimental import pallas as pl
from jax.experimental.pallas import tpu as pltpu

D_MODEL = 2048
N_META = 16
HEAD_DIM = 128
N_HEADS = 8
W_GRP = N_HEADS * HEAD_DIM
W_QKV = 6 * W_GRP
D_FF = 11 * D_MODEL // 4
EPS = 1e-6
SCALE = HEAD_DIM ** -0.5

LANE = 128
SUBLANE = 8
BLK = 128
PAD = BLK - N_META
NEG = -1e30
VMEM_LIMIT = 56 * 1024 * 1024

f32 = jnp.float32
bf16 = jnp.bfloat16


def _rms(x, g):
    return x * lax.rsqrt(jnp.mean(x * x, axis=-1, keepdims=True) + EPS) * g


def _softplus(z):
    return jnp.maximum(z, 0.0) + jnp.log(1.0 + jnp.exp(-jnp.abs(z)))


def _split2(x):
    hi = x.astype(bf16)
    lo = (x - hi.astype(f32)).astype(bf16)
    return hi, lo


def _split3(x):
    h1 = x.astype(bf16)
    r = x - h1.astype(f32)
    h2 = r.astype(bf16)
    h3 = (r - h2.astype(f32)).astype(bf16)
    return h1, h2, h3


def _inproj_kernel(h_ref, g_ref, w_ref, wf_ref, bf_ref, qkv_ref, lf_ref, u_scr):
    j = pl.program_id(1)

    @pl.when(j == 0)
    def _():
        u = _rms(h_ref[...], g_ref[...]).astype(bf16)
        u_scr[...] = u
        f = jnp.dot(u, wf_ref[...], preferred_element_type=f32) + bf_ref[...]
        lf_ref[...] = -_softplus(-f)

    acc = jnp.dot(u_scr[...], w_ref[...], preferred_element_type=f32)
    is_q = jnp.logical_or(j == 0, j == 3)
    qkv_ref[...] = (acc * jnp.where(is_q, SCALE, 1.0)).astype(bf16)


def _inproj(h, g, w_qkv, w_f, b_f, tm=512):
    rows = h.shape[0]
    tn = W_GRP
    return pl.pallas_call(
        _inproj_kernel,
        grid=(rows // tm, W_QKV // tn),
        in_specs=[
            pl.BlockSpec((tm, D_MODEL), lambda i, j: (i, 0)),
            pl.BlockSpec((1, D_MODEL), lambda i, j: (0, 0)),
            pl.BlockSpec((D_MODEL, tn), lambda i, j: (0, j)),
            pl.BlockSpec((D_MODEL, LANE), lambda i, j: (0, 0)),
            pl.BlockSpec((1, LANE), lambda i, j: (0, 0)),
        ],
        out_specs=[
            pl.BlockSpec((tm, tn), lambda i, j: (i, j)),
            pl.BlockSpec((tm, LANE), lambda i, j: (i, 0)),
        ],
        out_shape=[
            jax.ShapeDtypeStruct((rows, W_QKV), bf16),
            jax.ShapeDtypeStruct((rows, LANE), f32),
        ],
        scratch_shapes=[pltpu.VMEM((tm, D_MODEL), bf16)],
        compiler_params=pltpu.CompilerParams(
            dimension_semantics=("parallel", "arbitrary"), vmem_limit_bytes=VMEM_LIMIT),
        name="inproj",
    )(h, g, w_qkv, w_f, b_f)


def _cumsum_kernel(lf_ref, ct_ref, carry):
    i = pl.program_id(1)

    @pl.when(i == 0)
    def _():
        carry[...] = jnp.zeros_like(carry)

    s = lax.broadcasted_iota(jnp.int32, (BLK, BLK), 0)
    t = lax.broadcasted_iota(jnp.int32, (BLK, BLK), 1)
    tri = (s <= t).astype(bf16)
    lft = lf_ref[...].T
    tot = carry[...]
    for piece in _split3(lft):
        tot = tot + jnp.dot(piece, tri, preferred_element_type=f32)
    ct_ref[0, 0] = tot[:SUBLANE]
    carry[...] = jnp.broadcast_to(tot[:, BLK - 1:BLK], (LANE, BLK))


def _cumsum(lf, batch, lp):
    nb = lp // BLK
    return pl.pallas_call(
        _cumsum_kernel,
        grid=(batch, nb),
        in_specs=[pl.BlockSpec((BLK, LANE), lambda b, i: (b * nb + i, 0))],
        out_specs=pl.BlockSpec((1, 1, SUBLANE, BLK), lambda b, i: (b, i, 0, 0)),
        out_shape=jax.ShapeDtypeStruct((batch, nb, SUBLANE, BLK), f32),
        scratch_shapes=[pltpu.VMEM((LANE, BLK), f32)],
        compiler_params=pltpu.CompilerParams(dimension_semantics=("parallel", "arbitrary")),
        name="cumsum",
    )(lf)


def _sb_kernel(q_ref, k_ref, v_ref, g_ref, o_ref):
    i = pl.program_id(2)
    q = q_ref[...]
    rows = lax.broadcasted_iota(jnp.int32, (BLK, BLK), 0)
    cols = lax.broadcasted_iota(jnp.int32, (BLK, BLK), 1)
    later = (rows > cols).astype(bf16)
    t_pos = i * BLK + rows

    def body(jj, state):
        carry, acc = state
        j = i - jj
        off = pl.multiple_of(j * BLK, BLK)
        k = k_ref[pl.ds(off, BLK), :]
        v = v_ref[pl.ds(off, BLK), :]
        z = lax.dot_general(q, k, (((1,), (1,)), ((), ())), preferred_element_type=f32)
        s_pos = j * BLK + cols
        valid = jnp.logical_and(s_pos < t_pos, s_pos >= PAD)
        sp = _softplus(z)
        spm = jnp.where(valid, sp, 0.0)
        hi, lo = _split2(spm)
        between = (jnp.dot(hi, later, preferred_element_type=f32)
                   + jnp.dot(lo, later, preferred_element_type=f32))
        a = jnp.where(valid, jnp.exp(z - sp - between - carry), 0.0)
        acc = acc + jnp.dot(a.astype(bf16), v, preferred_element_type=f32)
        carry = carry + jnp.sum(spm, axis=1, keepdims=True)
        return carry, acc

    init = (jnp.zeros((BLK, 1), f32), jnp.zeros((BLK, HEAD_DIM), f32))
    _, acc = lax.fori_loop(0, i + 1, body, init)
    o_ref[...] = _rms(acc, g_ref[0]).astype(bf16)


def _sb_attention(qkv, g, batch, lp):
    nb = lp // BLK
    return pl.pallas_call(
        _sb_kernel,
        grid=(batch, N_HEADS, nb),
        in_specs=[
            pl.BlockSpec((BLK, HEAD_DIM), lambda b, h, i: (b * nb + i, h)),
            pl.BlockSpec((lp, HEAD_DIM), lambda b, h, i: (b, N_HEADS + h)),
            pl.BlockSpec((lp, HEAD_DIM), lambda b, h, i: (b, 2 * N_HEADS + h)),
            pl.BlockSpec((1, 1, HEAD_DIM), lambda b, h, i: (h, 0, 0)),
        ],
        out_specs=pl.BlockSpec((BLK, HEAD_DIM), lambda b, h, i: (b * nb + i, h)),
        out_shape=jax.ShapeDtypeStruct((batch * lp, W_GRP), bf16),
        compiler_params=pltpu.CompilerParams(
            dimension_semantics=("parallel", "parallel", "arbitrary"), vmem_limit_bytes=VMEM_LIMIT),
        name="sb_attention",
    )(qkv, qkv, qkv, g)


def _fox_kernel(q_ref, k_ref, v_ref, ct_ref, g_ref, o_ref):
    i = pl.program_id(2)
    h = pl.program_id(1)
    q = q_ref[...]
    rows = lax.broadcasted_iota(jnp.int32, (BLK, BLK), 0)
    cols = lax.broadcasted_iota(jnp.int32, (BLK, BLK), 1)
    t_pos = i * BLK + rows

    def body(j, state):
        m, l, acc = state
        off = pl.multiple_of(j * BLK, BLK)
        k = k_ref[pl.ds(off, BLK), :]
        v = v_ref[pl.ds(off, BLK), :]
        z = lax.dot_general(q, k, (((1,), (1,)), ((), ())), preferred_element_type=f32)
        c_s = ct_ref[0, j, pl.ds(h, 1), :]
        s_pos = j * BLK + cols
        valid = jnp.logical_and(s_pos <= t_pos, s_pos >= PAD)
        logit = jnp.where(valid, z - c_s, NEG)
        m_new = jnp.maximum(m, jnp.max(logit, axis=1, keepdims=True))
        alpha = jnp.exp(m - m_new)
        p = jnp.where(valid, jnp.exp(logit - m_new), 0.0)
        l = alpha * l + jnp.sum(p, axis=1, keepdims=True)
        acc = alpha * acc + jnp.dot(p.astype(bf16), v, preferred_element_type=f32)
        return m_new, l, acc

    init = (jnp.full((BLK, 1), NEG, f32), jnp.zeros((BLK, 1), f32), jnp.zeros((BLK, HEAD_DIM), f32))
    _, l, acc = lax.fori_loop(0, i + 1, body, init)
    o = acc / jnp.where(l > 0.0, l, 1.0)
    o_ref[...] = _rms(o, g_ref[0]).astype(bf16)


def _fox_attention(qkv, ct, g, batch, lp):
    nb = lp // BLK
    return pl.pallas_call(
        _fox_kernel,
        grid=(batch, N_HEADS, nb),
        in_specs=[
            pl.BlockSpec((BLK, HEAD_DIM), lambda b, h, i: (b * nb + i, 3 * N_HEADS + h)),
            pl.BlockSpec((lp, HEAD_DIM), lambda b, h, i: (b, 4 * N_HEADS + h)),
            pl.BlockSpec((lp, HEAD_DIM), lambda b, h, i: (b, 5 * N_HEADS + h)),
            pl.BlockSpec((1, nb, SUBLANE, BLK), lambda b, h, i: (b, 0, 0, 0)),
            pl.BlockSpec((1, 1, HEAD_DIM), lambda b, h, i: (h, 0, 0)),
        ],
        out_specs=pl.BlockSpec((BLK, HEAD_DIM), lambda b, h, i: (b * nb + i, h)),
        out_shape=jax.ShapeDtypeStruct((batch * lp, W_GRP), bf16),
        compiler_params=pltpu.CompilerParams(
            dimension_semantics=("parallel", "parallel", "arbitrary"), vmem_limit_bytes=VMEM_LIMIT),
        name="fox_attention",
    )(qkv, qkv, qkv, ct, g)


def _row_is_token(i, tm, lp, rows):
    r = i * tm + lax.broadcasted_iota(jnp.int32, (tm, 1), 0)
    is_pad = jnp.zeros((tm, 1), jnp.bool_)
    for start in range(0, rows, lp):
        is_pad = jnp.logical_or(is_pad, jnp.logical_and(r >= start, r < start + PAD))
    return jnp.logical_not(is_pad)


def _outproj_kernel(osb_ref, ofx_ref, wa_ref, wb_ref, h_ref, g_ref, out_ref, *, tm, lp, rows):
    mix = (jnp.dot(osb_ref[...], wa_ref[...], preferred_element_type=f32)
           + jnp.dot(ofx_ref[...], wb_ref[...], preferred_element_type=f32))
    hn = h_ref[...] + _rms(mix, g_ref[...])
    out_ref[...] = jnp.where(_row_is_token(pl.program_id(0), tm, lp, rows), hn, 0.0)


def _outproj(o_sb, o_fx, w_a, w_b, h, g, lp, tm=512):
    rows = h.shape[0]
    return pl.pallas_call(
        partial(_outproj_kernel, tm=tm, lp=lp, rows=rows),
        grid=(rows // tm,),
        in_specs=[
            pl.BlockSpec((tm, W_GRP), lambda i: (i, 0)),
            pl.BlockSpec((tm, W_GRP), lambda i: (i, 0)),
            pl.BlockSpec((W_GRP, D_MODEL), lambda i: (0, 0)),
            pl.BlockSpec((W_GRP, D_MODEL), lambda i: (0, 0)),
            pl.BlockSpec((tm, D_MODEL), lambda i: (i, 0)),
            pl.BlockSpec((1, D_MODEL), lambda i: (0, 0)),
        ],
        out_specs=pl.BlockSpec((tm, D_MODEL), lambda i: (i, 0)),
        out_shape=jax.ShapeDtypeStruct((rows, D_MODEL), f32),
        compiler_params=pltpu.CompilerParams(
            dimension_semantics=("parallel",), vmem_limit_bytes=VMEM_LIMIT),
        name="outproj",
    )(o_sb, o_fx, w_a, w_b, h, g)


def _upconv_kernel(h_ref, g_ref, wg_ref, wu_ref, cwg_ref, cwu_ref, cbg_ref, cbu_ref, out_ref,
                   u_scr, halo_scr, ag_scr, au_scr, *, tm):
    i = pl.program_id(0)
    j = pl.program_id(1)

    @pl.when(j == 0)
    def _():
        u_scr[...] = _rms(h_ref[...], g_ref[...]).astype(bf16)

    @pl.when(i == 0)
    def _():
        halo_scr[j] = jnp.zeros(halo_scr.shape[1:], f32)

    u = u_scr[...]

    def conv(w_ref, cw_ref, cb_ref, scr, slot):
        a = jnp.dot(u, w_ref[...], preferred_element_type=f32)
        scr[pl.ds(0, SUBLANE), :] = halo_scr[j, slot]
        scr[pl.ds(SUBLANE, tm), :] = a
        halo_scr[j, slot] = a[tm - SUBLANE:, :]
        cw = cw_ref[...]
        return (cw[0:1] * scr[pl.ds(SUBLANE - 2, tm), :] + cw[1:2] * scr[pl.ds(SUBLANE - 1, tm), :]
                + cw[2:3] * a + cb_ref[...])

    gate = conv(wg_ref, cwg_ref, cbg_ref, ag_scr, 0)
    up = conv(wu_ref, cwu_ref, cbu_ref, au_scr, 1)
    out_ref[...] = (gate * (1.0 / (1.0 + jnp.exp(-gate))) * up).astype(bf16)


def _upconv(h, g, w_up, conv_w, conv_b, tm=512, tn=512):
    rows = h.shape[0]
    nn = D_FF // tn
    return pl.pallas_call(
        partial(_upconv_kernel, tm=tm),
        grid=(rows // tm, nn),
        in_specs=[
            pl.BlockSpec((tm, D_MODEL), lambda i, j: (i, 0)),
            pl.BlockSpec((1, D_MODEL), lambda i, j: (0, 0)),
            pl.BlockSpec((D_MODEL, tn), lambda i, j: (0, j)),
            pl.BlockSpec((D_MODEL, tn), lambda i, j: (0, nn + j)),
            pl.BlockSpec((3, tn), lambda i, j: (0, j)),
            pl.BlockSpec((3, tn), lambda i, j: (0, nn + j)),
            pl.BlockSpec((1, tn), lambda i, j: (0, j)),
            pl.BlockSpec((1, tn), lambda i, j: (0, nn + j)),
        ],
        out_specs=pl.BlockSpec((tm, tn), lambda i, j: (i, j)),
        out_shape=jax.ShapeDtypeStruct((rows, D_FF), bf16),
        scratch_shapes=[
            pltpu.VMEM((tm, D_MODEL), bf16),
            pltpu.VMEM((nn, 2, SUBLANE, tn), f32),
            pltpu.VMEM((tm + SUBLANE, tn), f32),
            pltpu.VMEM((tm + SUBLANE, tn), f32),
        ],
        compiler_params=pltpu.CompilerParams(
            dimension_semantics=("arbitrary", "arbitrary"), vmem_limit_bytes=VMEM_LIMIT),
        name="upconv",
    )(h, g, w_up, w_up, conv_w, conv_w, conv_b, conv_b)


def _downproj_kernel(a_ref, w_ref, h_ref, g_ref, out_ref, *, tm, lp, rows):
    ff = jnp.dot(a_ref[...], w_ref[...], preferred_element_type=f32)
    hn = h_ref[...] + _rms(ff, g_ref[...])
    out_ref[...] = jnp.where(_row_is_token(pl.program_id(0), tm, lp, rows), hn, 0.0)


def _downproj(a, w_down, h, g, lp, tm=256):
    rows = h.shape[0]
    return pl.pallas_call(
        partial(_downproj_kernel, tm=tm, lp=lp, rows=rows),
        grid=(rows // tm,),
        in_specs=[
            pl.BlockSpec((tm, D_FF), lambda i: (i, 0)),
            pl.BlockSpec((D_FF, D_MODEL), lambda i: (0, 0), pipeline_mode=pl.Buffered(1)),
            pl.BlockSpec((tm, D_MODEL), lambda i: (i, 0)),
            pl.BlockSpec((1, D_MODEL), lambda i: (0, 0)),
        ],
        out_specs=pl.BlockSpec((tm, D_MODEL), lambda i: (i, 0)),
        out_shape=jax.ShapeDtypeStruct((rows, D_MODEL), f32),
        compiler_params=pltpu.CompilerParams(
            dimension_semantics=("parallel",), vmem_limit_bytes=VMEM_LIMIT),
        name="downproj",
    )(a, w_down, h, g)


def kernel(x, meta, g_mix_pre, w_in, b_f, g_sb, g_fox, w_out, g_mix_post, g_ffn_pre, w_up, conv_w, conv_b,
           w_down, g_ffn_post):
    batch, seq, _ = x.shape
    depth = w_in.shape[0]
    lp = PAD + N_META + seq
    h = jnp.concatenate([
        jnp.zeros((batch, PAD, D_MODEL), x.dtype),
        jnp.broadcast_to(meta[None].astype(x.dtype), (batch, N_META, D_MODEL)),
        x], axis=1).reshape(batch * lp, D_MODEL)
    for d in range(depth):
        w_qkv = w_in[d, :, :W_QKV].astype(bf16)
        w_f = jnp.pad(w_in[d, :, W_QKV:], ((0, 0), (0, LANE - N_HEADS))).astype(bf16)
        bias_f = jnp.pad(b_f[d], (0, LANE - N_HEADS)).reshape(1, LANE)
        qkv, lf = _inproj(h, g_mix_pre[d].reshape(1, D_MODEL), w_qkv, w_f, bias_f)
        ct = _cumsum(lf, batch, lp)
        o_sb = _sb_attention(qkv, g_sb[d].reshape(N_HEADS, 1, HEAD_DIM), batch, lp)
        o_fx = _fox_attention(qkv, ct, g_fox[d].reshape(N_HEADS, 1, HEAD_DIM), batch, lp)
        w_o = w_out[d].astype(bf16)
        h = _outproj(o_sb, o_fx, w_o[:W_GRP], w_o[W_GRP:], h, g_mix_post[d].reshape(1, D_MODEL), lp)
        a = _upconv(h, g_ffn_pre[d].reshape(1, D_MODEL), w_up[d].astype(bf16), conv_w[d],
                    conv_b[d].reshape(1, 2 * D_FF))
        h = _downproj(a, w_down[d].astype(bf16), h, g_ffn_post[d].reshape(1, D_MODEL), lp)
    return h.reshape(batch, lp, D_MODEL)[:, PAD + N_META:]
```

```python
from functools import partial

import jax
import jax.numpy as jnp
from jax import lax
from jax.experimental import pallas as pl
from jax.experimental.pallas import tpu as pltpu

D_MODEL = 2048
N_META = 16
HEAD_DIM = 128
N_HEADS = 8
W_GRP = N_HEADS * HEAD_DIM
W_QKV = 6 * W_GRP
D_FF = 11 * D_MODEL // 4
EPS = 1e-6
SCALE = HEAD_DIM ** -0.5

LANE = 128
SUBLANE = 8
BLK = 128
TQ = 256
PAD = BLK - N_META
NEG = -1e30
VMEM_LIMIT = 56 * 1024 * 1024

f32 = jnp.float32
bf16 = jnp.bfloat16


def _rms(x, g):
    return x * lax.rsqrt(jnp.mean(x * x, axis=-1, keepdims=True) + EPS) * g


def _softplus(z):
    return jnp.maximum(z, 0.0) + jnp.log(1.0 + jnp.exp(-jnp.abs(z)))


def _split2(x):
    hi = x.astype(bf16)
    lo = (x - hi.astype(f32)).astype(bf16)
    return hi, lo


def _split3(x):
    h1 = x.astype(bf16)
    r = x - h1.astype(f32)
    h2 = r.astype(bf16)
    h3 = (r - h2.astype(f32)).astype(bf16)
    return h1, h2, h3


def _inproj_kernel(h_ref, g_ref, w_ref, wf_ref, bf_ref, qkv_ref, lf_ref, u_scr):
    j = pl.program_id(1)

    @pl.when(j == 0)
    def _():
        u = _rms(h_ref[...], g_ref[...]).astype(bf16)
        u_scr[...] = u
        f = jnp.dot(u, wf_ref[...], preferred_element_type=f32) + bf_ref[...]
        lf_ref[...] = -_softplus(-f)

    acc = jnp.dot(u_scr[...], w_ref[...], preferred_element_type=f32)
    is_q = jnp.logical_or(j == 0, j == 3)
    qkv_ref[...] = (acc * jnp.where(is_q, SCALE, 1.0)).astype(bf16)


def _inproj(h, g, w_qkv, w_f, b_f, tm=512):
    rows = h.shape[0]
    tn = W_GRP
    return pl.pallas_call(
        _inproj_kernel,
        grid=(rows // tm, W_QKV // tn),
        in_specs=[
            pl.BlockSpec((tm, D_MODEL), lambda i, j: (i, 0)),
            pl.BlockSpec((1, D_MODEL), lambda i, j: (0, 0)),
            pl.BlockSpec((D_MODEL, tn), lambda i, j: (0, j)),
            pl.BlockSpec((D_MODEL, LANE), lambda i, j: (0, 0)),
            pl.BlockSpec((1, LANE), lambda i, j: (0, 0)),
        ],
        out_specs=[
            pl.BlockSpec((tm, tn), lambda i, j: (i, j)),
            pl.BlockSpec((tm, LANE), lambda i, j: (i, 0)),
        ],
        out_shape=[
            jax.ShapeDtypeStruct((rows, W_QKV), bf16),
            jax.ShapeDtypeStruct((rows, LANE), f32),
        ],
        scratch_shapes=[pltpu.VMEM((tm, D_MODEL), bf16)],
        compiler_params=pltpu.CompilerParams(
            dimension_semantics=("parallel", "arbitrary"), vmem_limit_bytes=VMEM_LIMIT),
        name="inproj",
    )(h, g, w_qkv, w_f, b_f)


def _cumsum_kernel(lf_ref, c_ref, carry):
    i = pl.program_id(1)

    @pl.when(i == 0)
    def _():
        carry[...] = jnp.zeros_like(carry)

    t = lax.broadcasted_iota(jnp.int32, (BLK, BLK), 0)
    s = lax.broadcasted_iota(jnp.int32, (BLK, BLK), 1)
    tri = (s <= t).astype(bf16)
    tot = jnp.broadcast_to(carry[0:1, :], (BLK, LANE))
    for piece in _split3(lf_ref[...]):
        tot = tot + jnp.dot(tri, piece, preferred_element_type=f32)
    c_ref[...] = tot
    carry[...] = jnp.broadcast_to(tot[BLK - 1:BLK, :], carry.shape)


def _cumsum(lf, batch, lp):
    nb = lp // BLK
    return pl.pallas_call(
        _cumsum_kernel,
        grid=(batch, nb),
        in_specs=[pl.BlockSpec((BLK, LANE), lambda b, i: (b * nb + i, 0))],
        out_specs=pl.BlockSpec((BLK, LANE), lambda b, i: (b * nb + i, 0)),
        out_shape=jax.ShapeDtypeStruct((batch * lp, LANE), f32),
        scratch_shapes=[pltpu.VMEM((SUBLANE, LANE), f32)],
        compiler_params=pltpu.CompilerParams(dimension_semantics=("parallel", "arbitrary")),
        name="cumsum",
    )(lf)


def _later2():
    s = lax.broadcasted_iota(jnp.int32, (BLK, 2 * BLK), 0)
    j = lax.broadcasted_iota(jnp.int32, (BLK, 2 * BLK), 1)
    return (jnp.bitwise_and(j, BLK - 1) > s).astype(bf16)


def _key_query_iota(tk, tq):
    return (lax.broadcasted_iota(jnp.int32, (tk, tq), 0), lax.broadcasted_iota(jnp.int32, (tk, tq), 1))


def _head_cols(hb):
    return [slice(hh * HEAD_DIM, (hh + 1) * HEAD_DIM) for hh in range(hb)]


def _store_vt(v_ref, vt_scr, hb, nblk):
    def body(c, _):
        off = pl.multiple_of(c * BLK, BLK)
        for hh, cs in enumerate(_head_cols(hb)):
            vt_scr[hh, c] = v_ref[pl.ds(off, BLK), cs].astype(f32).T.astype(bf16)
        return 0
    lax.fori_loop(0, nblk, body, 0)


def _vt_chunk(vt_scr, hh, blk0, tk):
    parts = [vt_scr[hh, blk0 + s] for s in range(tk // BLK)]
    return jnp.concatenate(parts, axis=1) if len(parts) > 1 else parts[0]


def _store_qt(q_ref, qt_scr, t0, tq, hb):
    for hh, cs in enumerate(_head_cols(hb)):
        qt_scr[hh, :, 0:tq] = q_ref[pl.ds(t0, tq), cs].astype(f32).T.astype(bf16)


def _scores(k_ref, qt_scr, row0, tk, tq, hb):
    return [jnp.dot(k_ref[pl.ds(row0, tk), cs], qt_scr[hh, :, 0:tq], preferred_element_type=f32)
            for hh, cs in enumerate(_head_cols(hb))]


def _finish_head(acc_t, g_row):
    inv = lax.rsqrt(jnp.mean(acc_t * acc_t, axis=0, keepdims=True) + EPS)
    return ((acc_t * inv).T * g_row).astype(bf16)


def _sb_chunk(zs, vts, runs, valid, later2):
    nsub = zs[0].shape[0] // BLK
    ws, stacks, firsts = [], [], []
    for z in zs:
        sp = _softplus(z)
        ws.append(z - sp)
        spm = sp if valid is None else jnp.where(valid, sp, 0.0)
        hi, lo = _split2(spm)
        stacks.append([jnp.concatenate([hi[s * BLK:(s + 1) * BLK], lo[s * BLK:(s + 1) * BLK]], axis=0)
                       for s in range(nsub)])
        firsts.append([spm[s * BLK:s * BLK + 1] for s in range(nsub)])
    withins = [[jnp.dot(later2, st, preferred_element_type=f32) for st in stack] for stack in stacks]
    new_runs, outs = [], []
    for w, within, first, run, vt in zip(ws, withins, firsts, runs, vts):
        totals = [None] * nsub
        for s in reversed(range(nsub)):
            totals[s] = within[s] + run
            run = run + within[s][0:1] + first[s]
        total = jnp.concatenate(totals, axis=0) if nsub > 1 else totals[0]
        a = jnp.exp(w - total)
        if valid is not None:
            a = jnp.where(valid, a, 0.0)
        outs.append(jnp.dot(vt, a.astype(bf16), preferred_element_type=f32))
        new_runs.append(run)
    return new_runs, outs


def _sb_tile(i, tq, q_ref, k_ref, g_ref, o_ref, vt_scr, qt_scr, z_scr, acc_scr, later2, hb):
    nsub = TQ // BLK
    t0 = pl.multiple_of(i * TQ, TQ)
    _store_qt(q_ref, qt_scr, t0, tq, hb)
    kr, qc = _key_query_iota(tq, tq)
    zs = _scores(k_ref, qt_scr, t0, tq, tq, hb)
    nxt = _scores(k_ref, qt_scr, pl.multiple_of(t0 - TQ, TQ), TQ, tq, hb)
    runs, outs = _sb_chunk(zs, [_vt_chunk(vt_scr, hh, i * nsub, tq) for hh in range(hb)],
                           [jnp.zeros((1, tq), f32)] * hb, kr < qc, later2)
    for hh in range(hb):
        acc_scr[hh, :, 0:tq] = outs[hh]
        z_scr[hh, :, 0:tq] = nxt[hh]

    def body(n, runs):
        c = i - 1 - n
        row0 = pl.multiple_of(c * TQ, TQ)
        zs = [z_scr[hh, :, 0:tq] for hh in range(hb)]
        nxt = _scores(k_ref, qt_scr, pl.multiple_of(row0 - TQ, TQ), TQ, tq, hb)
        runs, outs = _sb_chunk(zs, [_vt_chunk(vt_scr, hh, c * nsub, TQ) for hh in range(hb)], list(runs), None,
                               later2)
        for hh in range(hb):
            acc_scr[hh, :, 0:tq] += outs[hh]
            z_scr[hh, :, 0:tq] = nxt[hh]
        return tuple(runs)

    runs = lax.fori_loop(0, i - 1, body, tuple(runs))
    kr0, _ = _key_query_iota(TQ, tq)
    zs = [z_scr[hh, :, 0:tq] for hh in range(hb)]
    _, outs = _sb_chunk(zs, [_vt_chunk(vt_scr, hh, 0, TQ) for hh in range(hb)], list(runs), kr0 >= PAD, later2)
    for hh, cs in enumerate(_head_cols(hb)):
        o_ref[pl.ds(t0, tq), cs] = _finish_head(acc_scr[hh, :, 0:tq] + outs[hh], g_ref[hh])


def _sb_kernel(q_ref, k_ref, v_ref, g_ref, o_ref, vt_scr, qt_scr, z_scr, acc_scr, *, hb, nblk, nq, tq_last):
    i = pl.program_id(2)
    later2 = _later2()
    tile = partial(_sb_tile, q_ref=q_ref, k_ref=k_ref, g_ref=g_ref, o_ref=o_ref, vt_scr=vt_scr, qt_scr=qt_scr,
                   z_scr=z_scr, acc_scr=acc_scr, later2=later2, hb=hb)

    @pl.when(i == 0)
    def _():
        _store_vt(v_ref, vt_scr, hb, nblk)
        _store_qt(q_ref, qt_scr, 0, TQ, hb)
        kr, qc = _key_query_iota(TQ, TQ)
        valid = jnp.logical_and(kr < qc, kr >= PAD)
        zs = _scores(k_ref, qt_scr, 0, TQ, TQ, hb)
        _, outs = _sb_chunk(zs, [_vt_chunk(vt_scr, hh, 0, TQ) for hh in range(hb)],
                            [jnp.zeros((1, TQ), f32)] * hb, valid, later2)
        for hh, cs in enumerate(_head_cols(hb)):
            o_ref[0:TQ, cs] = _finish_head(outs[hh], g_ref[hh])

    @pl.when(jnp.logical_and(i > 0, i < nq - 1))
    def _():
        tile(i, TQ)

    @pl.when(i == nq - 1)
    def _():
        tile(i, tq_last)


def _attn_specs(lp, hb, first_col):
    ng = N_HEADS // hb
    col = lambda part: pl.BlockSpec((lp, hb * HEAD_DIM), lambda b, g, i: (b, (first_col + part) * ng + g))
    return ng, [col(0), col(1), col(2)]


def _attn_tiles(lp):
    nq = pl.cdiv(lp, TQ)
    return nq, lp - (nq - 1) * TQ


def _sb_attention(qkv, g, batch, lp, hb=2):
    ng, qkv_specs = _attn_specs(lp, hb, 0)
    nblk = lp // BLK
    nq, tq_last = _attn_tiles(lp)
    return pl.pallas_call(
        partial(_sb_kernel, hb=hb, nblk=nblk, nq=nq, tq_last=tq_last),
        grid=(batch, ng, nq),
        in_specs=qkv_specs + [pl.BlockSpec((hb, 1, HEAD_DIM), lambda b, g, i: (g, 0, 0))],
        out_specs=pl.BlockSpec((lp, hb * HEAD_DIM), lambda b, g, i: (b, g)),
        out_shape=jax.ShapeDtypeStruct((batch * lp, W_GRP), bf16),
        scratch_shapes=[
            pltpu.VMEM((hb, nblk, HEAD_DIM, BLK), bf16),
            pltpu.VMEM((hb, HEAD_DIM, TQ), bf16),
            pltpu.VMEM((hb, TQ, TQ), f32),
            pltpu.VMEM((hb, HEAD_DIM, TQ), f32),
        ],
        compiler_params=pltpu.CompilerParams(
            dimension_semantics=("parallel", "parallel", "arbitrary"), vmem_limit_bytes=VMEM_LIMIT),
        name="sb_attention",
    )(qkv, qkv, qkv, g)


def _fox_chunk(zs, vts, cbs, ms, ls, valid):
    tq = zs[0].shape[1]
    logits = []
    for z, cb in zip(zs, cbs):
        logit = z - jnp.concatenate([cb] * (tq // LANE), axis=1)
        logits.append(logit if valid is None else jnp.where(valid, logit, NEG))
    new_ms, new_ls, alphas, outs = [], [], [], []
    for logit, m, l, vt in zip(logits, ms, ls, vts):
        m_new = jnp.maximum(m, jnp.max(logit, axis=0, keepdims=True))
        alpha = jnp.exp(m - m_new)
        p = jnp.exp(logit - m_new)
        if valid is not None:
            p = jnp.where(valid, p, 0.0)
        new_ls.append(alpha * l + jnp.sum(p, axis=0, keepdims=True))
        outs.append(jnp.dot(vt, p.astype(bf16), preferred_element_type=f32))
        new_ms.append(m_new)
        alphas.append(alpha)
    return new_ms, new_ls, alphas, outs


def _fox_finish(acc_t, l, g_row):
    return _finish_head(acc_t / jnp.where(l > 0.0, l, 1.0), g_row)


def _fox_tile(i, tq, q_ref, k_ref, g_ref, o_ref, vt_scr, cb_scr, qt_scr, z_scr, acc_scr, hb):
    nsub = TQ // BLK
    t0 = pl.multiple_of(i * TQ, TQ)
    _store_qt(q_ref, qt_scr, t0, tq, hb)
    kr, qc = _key_query_iota(tq, tq)
    zs = _scores(k_ref, qt_scr, t0, tq, tq, hb)
    nxt = _scores(k_ref, qt_scr, pl.multiple_of(t0 - TQ, TQ), TQ, tq, hb)
    ms, ls, _, outs = _fox_chunk(zs, [_vt_chunk(vt_scr, hh, i * nsub, tq) for hh in range(hb)],
                                 [cb_scr[hh, pl.ds(t0, tq), :] for hh in range(hb)],
                                 [jnp.full((1, tq), NEG, f32)] * hb, [jnp.zeros((1, tq), f32)] * hb, kr <= qc)
    for hh in range(hb):
        acc_scr[hh, :, 0:tq] = outs[hh]
        z_scr[hh, :, 0:tq] = nxt[hh]

    def step(c, state, last):
        row0 = pl.multiple_of(c * TQ, TQ)
        ms, ls = list(state[:hb]), list(state[hb:])
        zs = [z_scr[hh, :, 0:tq] for hh in range(hb)]
        if not last:
            nxt = _scores(k_ref, qt_scr, pl.multiple_of(row0 - TQ, TQ), TQ, tq, hb)
        ms, ls, alphas, outs = _fox_chunk(zs, [_vt_chunk(vt_scr, hh, c * nsub, TQ) for hh in range(hb)],
                                          [cb_scr[hh, pl.ds(row0, TQ), :] for hh in range(hb)], ms, ls, None)
        for hh in range(hb):
            acc_scr[hh, :, 0:tq] = alphas[hh] * acc_scr[hh, :, 0:tq] + outs[hh]
            if not last:
                z_scr[hh, :, 0:tq] = nxt[hh]
        return tuple(ms) + tuple(ls)

    state = lax.fori_loop(0, i - 1, lambda n, st: step(i - 1 - n, st, False), tuple(ms) + tuple(ls))
    state = step(0, state, True)
    for hh, cs in enumerate(_head_cols(hb)):
        o_ref[pl.ds(t0, tq), cs] = _fox_finish(acc_scr[hh, :, 0:tq], state[hb + hh], g_ref[hh])


def _fox_kernel(q_ref, k_ref, v_ref, c_ref, g_ref, o_ref, vt_scr, cb_scr, qt_scr, z_scr, acc_scr,
                *, hb, nblk, nq, tq_last):
    i = pl.program_id(2)
    head0 = pl.program_id(1) * hb
    tile = partial(_fox_tile, q_ref=q_ref, k_ref=k_ref, g_ref=g_ref, o_ref=o_ref, vt_scr=vt_scr, cb_scr=cb_scr,
                   qt_scr=qt_scr, z_scr=z_scr, acc_scr=acc_scr, hb=hb)

    @pl.when(i == 0)
    def _():
        _store_vt(v_ref, vt_scr, hb, nblk)
        row = lax.broadcasted_iota(jnp.int32, (BLK, LANE), 0)

        def body(c, _):
            off = pl.multiple_of(c * BLK, BLK)
            pieces = _split3(c_ref[pl.ds(off, BLK), :])
            for hh in range(hb):
                pick = (row == head0 + hh).astype(bf16)
                rep = sum(jnp.dot(p, pick, preferred_element_type=f32) for p in pieces)
                cb_scr[hh, pl.ds(off, BLK), :] = jnp.where(row + off < PAD, -NEG, rep)
            return 0
        lax.fori_loop(0, nblk, body, 0)

        _store_qt(q_ref, qt_scr, 0, TQ, hb)
        kr, qc = _key_query_iota(TQ, TQ)
        zs = _scores(k_ref, qt_scr, 0, TQ, TQ, hb)
        _, ls, _, outs = _fox_chunk(zs, [_vt_chunk(vt_scr, hh, 0, TQ) for hh in range(hb)],
                                    [cb_scr[hh, 0:TQ, :] for hh in range(hb)],
                                    [jnp.full((1, TQ), NEG, f32)] * hb, [jnp.zeros((1, TQ), f32)] * hb,
                                    jnp.logical_and(kr <= qc, kr >= PAD))
        for hh, cs in enumerate(_head_cols(hb)):
            o_ref[0:TQ, cs] = _fox_finish(outs[hh], ls[hh], g_ref[hh])

    @pl.when(jnp.logical_and(i > 0, i < nq - 1))
    def _():
        tile(i, TQ)

    @pl.when(i == nq - 1)
    def _():
        tile(i, tq_last)


def _fox_attention(qkv, c, g, batch, lp, hb=2):
    ng, qkv_specs = _attn_specs(lp, hb, 3)
    nblk = lp // BLK
    nq, tq_last = _attn_tiles(lp)
    return pl.pallas_call(
        partial(_fox_kernel, hb=hb, nblk=nblk, nq=nq, tq_last=tq_last),
        grid=(batch, ng, nq),
        in_specs=qkv_specs + [
            pl.BlockSpec((lp, LANE), lambda b, g, i: (b, 0)),
            pl.BlockSpec((hb, 1, HEAD_DIM), lambda b, g, i: (g, 0, 0)),
        ],
        out_specs=pl.BlockSpec((lp, hb * HEAD_DIM), lambda b, g, i: (b, g)),
        out_shape=jax.ShapeDtypeStruct((batch * lp, W_GRP), bf16),
        scratch_shapes=[
            pltpu.VMEM((hb, nblk, HEAD_DIM, BLK), bf16),
            pltpu.VMEM((hb, lp, LANE), f32),
            pltpu.VMEM((hb, HEAD_DIM, TQ), bf16),
            pltpu.VMEM((hb, TQ, TQ), f32),
            pltpu.VMEM((hb, HEAD_DIM, TQ), f32),
        ],
        compiler_params=pltpu.CompilerParams(
            dimension_semantics=("parallel", "parallel", "arbitrary"), vmem_limit_bytes=VMEM_LIMIT),
        name="fox_attention",
    )(qkv, qkv, qkv, c, g)


def _row_is_token(i, tm, lp, rows):
    r = i * tm + lax.broadcasted_iota(jnp.int32, (tm, 1), 0)
    is_pad = jnp.zeros((tm, 1), jnp.bool_)
    for start in range(0, rows, lp):
        is_pad = jnp.logical_or(is_pad, jnp.logical_and(r >= start, r < start + PAD))
    return jnp.logical_not(is_pad)


def _outproj_kernel(osb_ref, ofx_ref, wa_ref, wb_ref, h_ref, g_ref, out_ref, *, tm, lp, rows):
    mix = (jnp.dot(osb_ref[...], wa_ref[...], preferred_element_type=f32)
           + jnp.dot(ofx_ref[...], wb_ref[...], preferred_element_type=f32))
    hn = h_ref[...] + _rms(mix, g_ref[...])
    out_ref[...] = jnp.where(_row_is_token(pl.program_id(0), tm, lp, rows), hn, 0.0)


def _outproj(o_sb, o_fx, w_a, w_b, h, g, lp, tm=512):
    rows = h.shape[0]
    return pl.pallas_call(
        partial(_outproj_kernel, tm=tm, lp=lp, rows=rows),
        grid=(rows // tm,),
        in_specs=[
            pl.BlockSpec((tm, W_GRP), lambda i: (i, 0)),
            pl.BlockSpec((tm, W_GRP), lambda i: (i, 0)),
            pl.BlockSpec((W_GRP, D_MODEL), lambda i: (0, 0)),
            pl.BlockSpec((W_GRP, D_MODEL), lambda i: (0, 0)),
            pl.BlockSpec((tm, D_MODEL), lambda i: (i, 0)),
            pl.BlockSpec((1, D_MODEL), lambda i: (0, 0)),
        ],
        out_specs=pl.BlockSpec((tm, D_MODEL), lambda i: (i, 0)),
        out_shape=jax.ShapeDtypeStruct((rows, D_MODEL), f32),
        compiler_params=pltpu.CompilerParams(
            dimension_semantics=("parallel",), vmem_limit_bytes=VMEM_LIMIT),
        name="outproj",
    )(o_sb, o_fx, w_a, w_b, h, g)


def _upconv_kernel(h_ref, g_ref, wg_ref, wu_ref, cwg_ref, cwu_ref, cbg_ref, cbu_ref, out_ref,
                   u_scr, halo_scr, ag_scr, au_scr, *, tm):
    i = pl.program_id(0)
    j = pl.program_id(1)

    @pl.when(j == 0)
    def _():
        u_scr[...] = _rms(h_ref[...], g_ref[...]).astype(bf16)

    @pl.when(i == 0)
    def _():
        halo_scr[j] = jnp.zeros(halo_scr.shape[1:], f32)

    u = u_scr[...]

    def conv(w_ref, cw_ref, cb_ref, scr, slot):
        a = jnp.dot(u, w_ref[...], preferred_element_type=f32)
        scr[pl.ds(0, SUBLANE), :] = halo_scr[j, slot]
        scr[pl.ds(SUBLANE, tm), :] = a
        halo_scr[j, slot] = a[tm - SUBLANE:, :]
        cw = cw_ref[...]
        return (cw[0:1] * scr[pl.ds(SUBLANE - 2, tm), :] + cw[1:2] * scr[pl.ds(SUBLANE - 1, tm), :]
                + cw[2:3] * a + cb_ref[...])

    gate = conv(wg_ref, cwg_ref, cbg_ref, ag_scr, 0)
    up = conv(wu_ref, cwu_ref, cbu_ref, au_scr, 1)
    out_ref[...] = (gate * (1.0 / (1.0 + jnp.exp(-gate))) * up).astype(bf16)


def _upconv(h, g, w_up, conv_w, conv_b, tm=512, tn=512):
    rows = h.shape[0]
    nn = D_FF // tn
    return pl.pallas_call(
        partial(_upconv_kernel, tm=tm),
        grid=(rows // tm, nn),
        in_specs=[
            pl.BlockSpec((tm, D_MODEL), lambda i, j: (i, 0)),
            pl.BlockSpec((1, D_MODEL), lambda i, j: (0, 0)),
            pl.BlockSpec((D_MODEL, tn), lambda i, j: (0, j)),
            pl.BlockSpec((D_MODEL, tn), lambda i, j: (0, nn + j)),
            pl.BlockSpec((3, tn), lambda i, j: (0, j)),
            pl.BlockSpec((3, tn), lambda i, j: (0, nn + j)),
            pl.BlockSpec((1, tn), lambda i, j: (0, j)),
            pl.BlockSpec((1, tn), lambda i, j: (0, nn + j)),
        ],
        out_specs=pl.BlockSpec((tm, tn), lambda i, j: (i, j)),
        out_shape=jax.ShapeDtypeStruct((rows, D_FF), bf16),
        scratch_shapes=[
            pltpu.VMEM((tm, D_MODEL), bf16),
            pltpu.VMEM((nn, 2, SUBLANE, tn), f32),
            pltpu.VMEM((tm + SUBLANE, tn), f32),
            pltpu.VMEM((tm + SUBLANE, tn), f32),
        ],
        compiler_params=pltpu.CompilerParams(
            dimension_semantics=("arbitrary", "arbitrary"), vmem_limit_bytes=VMEM_LIMIT),
        name="upconv",
    )(h, g, w_up, w_up, conv_w, conv_w, conv_b, conv_b)


def _downproj_kernel(a_ref, w_ref, h_ref, g_ref, out_ref, *, tm, lp, rows):
    ff = jnp.dot(a_ref[...], w_ref[...], preferred_element_type=f32)
    hn = h_ref[...] + _rms(ff, g_ref[...])
    out_ref[...] = jnp.where(_row_is_token(pl.program_id(0), tm, lp, rows), hn, 0.0)


def _downproj(a, w_down, h, g, lp, tm=256):
    rows = h.shape[0]
    return pl.pallas_call(
        partial(_downproj_kernel, tm=tm, lp=lp, rows=rows),
        grid=(rows // tm,),
        in_specs=[
            pl.BlockSpec((tm, D_FF), lambda i: (i, 0)),
            pl.BlockSpec((D_FF, D_MODEL), lambda i: (0, 0), pipeline_mode=pl.Buffered(1)),
            pl.BlockSpec((tm, D_MODEL), lambda i: (i, 0)),
            pl.BlockSpec((1, D_MODEL), lambda i: (0, 0)),
        ],
        out_specs=pl.BlockSpec((tm, D_MODEL), lambda i: (i, 0)),
        out_shape=jax.ShapeDtypeStruct((rows, D_MODEL), f32),
        compiler_params=pltpu.CompilerParams(
            dimension_semantics=("parallel",), vmem_limit_bytes=VMEM_LIMIT),
        name="downproj",
    )(a, w_down, h, g)


def kernel(x, meta, g_mix_pre, w_in, b_f, g_sb, g_fox, w_out, g_mix_post, g_ffn_pre, w_up, conv_w, conv_b,
           w_down, g_ffn_post):
    batch, seq, _ = x.shape
    depth = w_in.shape[0]
    lp = PAD + N_META + seq
    h = jnp.concatenate([
        jnp.zeros((batch, PAD, D_MODEL), x.dtype),
        jnp.broadcast_to(meta[None].astype(x.dtype), (batch, N_META, D_MODEL)),
        x], axis=1).reshape(batch * lp, D_MODEL)
    for d in range(depth):
        w_qkv = w_in[d, :, :W_QKV].astype(bf16)
        w_f = jnp.pad(w_in[d, :, W_QKV:], ((0, 0), (0, LANE - N_HEADS))).astype(bf16)
        bias_f = jnp.pad(b_f[d], (0, LANE - N_HEADS)).reshape(1, LANE)
        qkv, lf = _inproj(h, g_mix_pre[d].reshape(1, D_MODEL), w_qkv, w_f, bias_f)
        c = _cumsum(lf, batch, lp)
        o_sb = _sb_attention(qkv, g_sb[d].reshape(N_HEADS, 1, HEAD_DIM), batch, lp)
        o_fx = _fox_attention(qkv, c, g_fox[d].reshape(N_HEADS, 1, HEAD_DIM), batch, lp)
        w_o = w_out[d].astype(bf16)
        h = _outproj(o_sb, o_fx, w_o[:W_GRP], w_o[W_GRP:], h, g_mix_post[d].reshape(1, D_MODEL), lp)
        a = _upconv(h, g_ffn_pre[d].reshape(1, D_MODEL), w_up[d].astype(bf16), conv_w[d],
                    conv_b[d].reshape(1, 2 * D_FF))
        h = _downproj(a, w_down[d].astype(bf16), h, g_ffn_post[d].reshape(1, D_MODEL), lp)
    return h.reshape(batch, lp, D_MODEL)[:, PAD + N_META:]
```

```python
import math
from functools import partial

import jax
import jax.numpy as jnp
from jax import lax
from jax.experimental import pallas as pl
from jax.experimental.pallas import tpu as pltpu

D_MODEL = 2048
N_META = 16
HEAD_DIM = 128
N_HEADS = 8
W_GRP = N_HEADS * HEAD_DIM
W_QKV = 6 * W_GRP
D_FF = 11 * D_MODEL // 4
EPS = 1e-6
LOG2E = math.log2(math.e)
Q_SCALE = HEAD_DIM ** -0.5 * LOG2E

LANE = 128
SUBLANE = 8
BLK = 128
TQ = 256
HB = 4
GW = HB * HEAD_DIM
PAD = BLK - N_META
NEG = -1e30
VMEM_LIMIT = 58 * 1024 * 1024

f32 = jnp.float32
bf16 = jnp.bfloat16


def _rms(x, g):
    return x * lax.rsqrt(jnp.mean(x * x, axis=-1, keepdims=True) + EPS) * g


def _softplus(z):
    return jnp.maximum(z, 0.0) + jnp.log(1.0 + jnp.exp(-jnp.abs(z)))


def _softplus2(z):
    neg_abs = lax.bitcast_convert_type(
        lax.bitcast_convert_type(z, jnp.uint32) | jnp.uint32(0x80000000), f32)
    return jnp.maximum(z, 0.0) + jnp.log(1.0 + jnp.exp2(neg_abs)) * LOG2E


def _split3(x):
    h1 = x.astype(bf16)
    r = x - h1.astype(f32)
    h2 = r.astype(bf16)
    h3 = (r - h2.astype(f32)).astype(bf16)
    return h1, h2, h3


def _inproj_kernel(h_ref, g_ref, w_ref, wf_ref, bf_ref, qkv_ref, lf_ref, u_scr):
    j = pl.program_id(1)

    @pl.when(j == 0)
    def _():
        u = _rms(h_ref[...], g_ref[...]).astype(bf16)
        u_scr[...] = u
        f = jnp.dot(u, wf_ref[...], preferred_element_type=f32) + bf_ref[...]
        lf_ref[...] = -_softplus(-f)

    acc = jnp.dot(u_scr[...], w_ref[...], preferred_element_type=f32)
    is_q = jnp.logical_or(j == 0, j == 3)
    out = (acc * jnp.where(is_q, Q_SCALE, 1.0)).astype(bf16)
    for g in range(qkv_ref.shape[0]):
        qkv_ref[g] = out[:, g * GW:(g + 1) * GW]


def _inproj(h, g, w_qkv, w_f, b_f, tm=512):
    rows = h.shape[0]
    tn = W_GRP
    return pl.pallas_call(
        _inproj_kernel,
        grid=(rows // tm, W_QKV // tn),
        in_specs=[
            pl.BlockSpec((tm, D_MODEL), lambda i, j: (i, 0)),
            pl.BlockSpec((1, D_MODEL), lambda i, j: (0, 0)),
            pl.BlockSpec((D_MODEL, tn), lambda i, j: (0, j)),
            pl.BlockSpec((D_MODEL, LANE), lambda i, j: (0, 0)),
            pl.BlockSpec((1, LANE), lambda i, j: (0, 0)),
        ],
        out_specs=[
            pl.BlockSpec((tn // GW, tm, GW), lambda i, j: (j, i, 0)),
            pl.BlockSpec((tm, LANE), lambda i, j: (i, 0)),
        ],
        out_shape=[
            jax.ShapeDtypeStruct((W_QKV // GW, rows, GW), bf16),
            jax.ShapeDtypeStruct((rows, LANE), f32),
        ],
        scratch_shapes=[pltpu.VMEM((tm, D_MODEL), bf16)],
        compiler_params=pltpu.CompilerParams(
            dimension_semantics=("parallel", "arbitrary"), vmem_limit_bytes=VMEM_LIMIT),
        name="inproj",
    )(h, g, w_qkv, w_f, b_f)


def _cumsum_kernel(lf_ref, c_ref, *, nblk):
    t = lax.broadcasted_iota(jnp.int32, (BLK, BLK), 0)
    s = lax.broadcasted_iota(jnp.int32, (BLK, BLK), 1)
    tri = (s <= t).astype(bf16)
    places = [jnp.logical_and(s == 3 * t + j, t < N_HEADS).astype(bf16) for j in range(3)]

    def body(i, carry):
        off = pl.multiple_of(i * BLK, BLK)
        tot = jnp.broadcast_to(carry, (BLK, LANE))
        for piece in _split3(lf_ref[pl.ds(off, BLK), :]):
            tot = tot + jnp.dot(tri, piece, preferred_element_type=f32)
        c2 = jnp.where(t + off < PAD, -NEG, tot * LOG2E)
        out = jnp.zeros((BLK, LANE), f32)
        for piece, place in zip(_split3(c2), places):
            out = out + jnp.dot(piece, place, preferred_element_type=f32)
        c_ref[pl.ds(off, BLK), :] = out.astype(bf16)
        return tot[BLK - 1:BLK, :]

    lax.fori_loop(0, nblk, body, jnp.zeros((1, LANE), f32))


def _cumsum(lf, batch, lp):
    return pl.pallas_call(
        partial(_cumsum_kernel, nblk=lp // BLK),
        grid=(batch,),
        in_specs=[pl.BlockSpec((lp, LANE), lambda b: (b, 0))],
        out_specs=pl.BlockSpec((lp, LANE), lambda b: (b, 0)),
        out_shape=jax.ShapeDtypeStruct((batch * lp, LANE), bf16),
        compiler_params=pltpu.CompilerParams(dimension_semantics=("parallel",)),
        name="cumsum",
    )(lf)


def _later():
    s = lax.broadcasted_iota(jnp.int32, (BLK, BLK), 0)
    j = lax.broadcasted_iota(jnp.int32, (BLK, BLK), 1)
    return (j > s).astype(bf16)


def _key_query_iota(tk, tq):
    return (lax.broadcasted_iota(jnp.int32, (tk, tq), 0), lax.broadcasted_iota(jnp.int32, (tk, tq), 1))


def _head_cols():
    return [slice(hh * HEAD_DIM, (hh + 1) * HEAD_DIM) for hh in range(HB)]


def _store_vt(v_ref, vt_scr, nblk):
    def body(c, _):
        off = pl.multiple_of(c * BLK, BLK)
        for hh, cs in enumerate(_head_cols()):
            vt_scr[hh, c] = v_ref[0, pl.ds(off, BLK), cs].astype(f32).T.astype(bf16)
        return 0
    lax.fori_loop(0, nblk, body, 0)


def _vt_chunk(vt_scr, hh, blk0, tk):
    parts = [vt_scr[hh, blk0 + s] for s in range(tk // BLK)]
    return jnp.concatenate(parts, axis=1) if len(parts) > 1 else parts[0]


def _store_minus_rows(qt_scr, hh, first_row, n_rows):
    r = lax.broadcasted_iota(jnp.int32, (LANE, TQ), 0)
    hit = jnp.logical_and(r >= first_row, r < first_row + n_rows)
    qt_scr[hh, HEAD_DIM:, :] = jnp.where(hit, -1.0, 0.0).astype(bf16)


def _store_qt(q_ref, qt_scr, t0, tq):
    for hh, cs in enumerate(_head_cols()):
        qt_scr[hh, 0:HEAD_DIM, 0:tq] = q_ref[0, pl.ds(t0, tq), cs].astype(f32).T.astype(bf16)


def _scores(k_ref, x_ref, qt_scr, row0, tk, tq):
    extra = x_ref[pl.ds(row0, tk), :]
    return [jnp.dot(jnp.concatenate([k_ref[0, pl.ds(row0, tk), cs], extra], axis=1), qt_scr[hh, :, 0:tq],
                    preferred_element_type=f32)
            for hh, cs in enumerate(_head_cols())]


def _finish_head(acc_t, g_row):
    inv = lax.rsqrt(jnp.mean(acc_t * acc_t, axis=0, keepdims=True) + EPS)
    return ((acc_t * inv).T * g_row).astype(bf16)


def _attn_tile(i, tq, chunk, init, fold, finish, q_ref, k_ref, x_ref, o_ref, vt_scr, qt_scr, z_scr, acc_scr):
    nsub = TQ // BLK
    t0 = pl.multiple_of(i * TQ, TQ)
    _store_qt(q_ref, qt_scr, t0, tq)
    zs = _scores(k_ref, x_ref, qt_scr, t0, tq, tq)
    nxt = _scores(k_ref, x_ref, qt_scr, pl.multiple_of(jnp.maximum(t0 - TQ, 0), TQ), TQ, tq)
    state, _, outs = chunk(zs, [_vt_chunk(vt_scr, hh, i * nsub, tq) for hh in range(HB)], init(tq), True)
    for hh in range(HB):
        acc_scr[hh, :, 0:tq] = outs[hh]
        z_scr[hh, :, 0:tq] = nxt[hh]

    def body(n, state):
        c = i - 1 - n
        zs = [z_scr[hh, :, 0:tq] for hh in range(HB)]
        nxt = _scores(k_ref, x_ref, qt_scr, pl.multiple_of(jnp.maximum(c - 1, 0) * TQ, TQ), TQ, tq)
        state, scales, outs = chunk(zs, [_vt_chunk(vt_scr, hh, c * nsub, TQ) for hh in range(HB)], state, False)
        for hh in range(HB):
            acc_scr[hh, :, 0:tq] = fold(acc_scr[hh, :, 0:tq], scales[hh], outs[hh])
            z_scr[hh, :, 0:tq] = nxt[hh]
        return state

    state = lax.fori_loop(0, i, body, state)
    for hh, cs in enumerate(_head_cols()):
        o_ref[0, pl.ds(t0, tq), cs] = finish(acc_scr[hh, :, 0:tq], state, hh)


def _attn_kernel_body(tile, v_ref, vt_scr, nblk, nq, tq_last, first_step):
    i = pl.program_id(2)

    @pl.when(i == 0)
    def _():
        _store_vt(v_ref, vt_scr, nblk)
        first_step()

    @pl.when(i < nq - 1)
    def _():
        tile(i, TQ)

    @pl.when(i == nq - 1)
    def _():
        tile(i, tq_last)


def _sb_chunk(zs, vts, runs, diag, later):
    tk, tq = zs[0].shape
    nsub = tk // BLK
    if diag:
        kr, qc = _key_query_iota(tk, tq)
        valid = kr < qc
    ws, his, firsts = [], [], []
    for z in zs:
        sp = _softplus2(z)
        ws.append(z - sp)
        spm = jnp.where(valid, sp, 0.0) if diag else sp
        hi = spm.astype(bf16)
        his.append([hi[s * BLK:(s + 1) * BLK] for s in range(nsub)])
        firsts.append([spm[s * BLK:s * BLK + 1] for s in range(nsub)])
    withins = [[jnp.dot(later, blk, preferred_element_type=f32) for blk in hi] for hi in his]
    new_runs, outs = [], []
    for w, within, first, run, vt in zip(ws, withins, firsts, runs, vts):
        totals = [None] * nsub
        for s in reversed(range(nsub)):
            totals[s] = within[s] + run
            run = run + within[s][0:1] + first[s]
        total = jnp.concatenate(totals, axis=0) if nsub > 1 else totals[0]
        a = jnp.exp2(w - total)
        if diag:
            a = jnp.where(valid, a, 0.0)
        outs.append(jnp.dot(vt, a.astype(bf16), preferred_element_type=f32))
        new_runs.append(run)
    return new_runs, [None] * len(zs), outs


def _sb_kernel(q_ref, k_ref, v_ref, x_ref, g_ref, o_ref, vt_scr, qt_scr, z_scr, acc_scr, *, nblk, nq, tq_last):
    later = _later()
    tile = partial(
        _attn_tile,
        chunk=lambda zs, vts, runs, diag: _sb_chunk(zs, vts, runs, diag, later),
        init=lambda tq: [jnp.zeros((1, tq), f32)] * HB,
        fold=lambda acc, scale, out: acc + out,
        finish=lambda acc, runs, hh: _finish_head(acc, g_ref[hh]),
        q_ref=q_ref, k_ref=k_ref, x_ref=x_ref, o_ref=o_ref, vt_scr=vt_scr, qt_scr=qt_scr, z_scr=z_scr,
        acc_scr=acc_scr)

    def first_step():
        for hh in range(HB):
            _store_minus_rows(qt_scr, hh, 0, 1)

    _attn_kernel_body(tile, v_ref, vt_scr, nblk, nq, tq_last, first_step)


def _attn_call(kernel_fn, name, qkv, extra, g, batch, lp, first_part):
    ng = N_HEADS // HB
    nblk = lp // BLK
    nq = pl.cdiv(lp, TQ)
    part = lambda p: pl.BlockSpec((1, lp, GW), lambda b, g, i: ((first_part + p) * ng + g, b, 0))
    return pl.pallas_call(
        partial(kernel_fn, nblk=nblk, nq=nq, tq_last=lp - (nq - 1) * TQ),
        grid=(batch, ng, nq),
        in_specs=[part(0), part(1), part(2), extra[1],
                  pl.BlockSpec((HB, 1, HEAD_DIM), lambda b, g, i: (g, 0, 0))],
        out_specs=pl.BlockSpec((1, lp, GW), lambda b, g, i: (g, b, 0)),
        out_shape=jax.ShapeDtypeStruct((ng, batch * lp, GW), bf16),
        scratch_shapes=[
            pltpu.VMEM((HB, nblk, HEAD_DIM, BLK), bf16),
            pltpu.VMEM((HB, HEAD_DIM + LANE, TQ), bf16),
            pltpu.VMEM((HB, TQ, TQ), f32),
            pltpu.VMEM((HB, HEAD_DIM, TQ), f32),
        ],
        compiler_params=pltpu.CompilerParams(
            dimension_semantics=("parallel", "parallel", "arbitrary"), vmem_limit_bytes=VMEM_LIMIT),
        name=name,
    )(qkv, qkv, qkv, extra[0], g)


def _sb_attention(qkv, pad_col, g, batch, lp):
    spec = pl.BlockSpec((lp, LANE), lambda b, g, i: (0, 0))
    return _attn_call(_sb_kernel, "sb_attention", qkv, (pad_col, spec), g, batch, lp, 0)


def _fox_chunk(zs, vts, state, diag):
    tk, tq = zs[0].shape
    nh = len(zs)
    ms, ls = state[:nh], state[nh:]
    if diag:
        kr, qc = _key_query_iota(tk, tq)
        valid = kr <= qc
        zs = [jnp.where(valid, z, NEG) for z in zs]
    new_ms = [jnp.maximum(m, jnp.max(z, axis=0, keepdims=True)) for z, m in zip(zs, ms)]
    alphas = [jnp.exp2(m - m_new) for m, m_new in zip(ms, new_ms)]
    new_ls, outs = [], []
    for z, m_new, alpha, l, vt in zip(zs, new_ms, alphas, ls, vts):
        p = jnp.exp2(z - m_new)
        if diag:
            p = jnp.where(valid, p, 0.0)
        new_ls.append(alpha * l + jnp.sum(p, axis=0, keepdims=True))
        outs.append(jnp.dot(vt, p.astype(bf16), preferred_element_type=f32))
    return new_ms + new_ls, alphas, outs


def _fox_kernel(q_ref, k_ref, v_ref, x_ref, g_ref, o_ref, vt_scr, qt_scr, z_scr, acc_scr, *, nblk, nq, tq_last):
    head0 = pl.program_id(1) * HB

    def finish(acc, state, hh):
        l = state[HB + hh]
        return _finish_head(acc / jnp.where(l > 0.0, l, 1.0), g_ref[hh])

    tile = partial(
        _attn_tile,
        chunk=_fox_chunk,
        init=lambda tq: [jnp.full((1, tq), NEG, f32)] * HB + [jnp.zeros((1, tq), f32)] * HB,
        fold=lambda acc, scale, out: scale * acc + out,
        finish=finish,
        q_ref=q_ref, k_ref=k_ref, x_ref=x_ref, o_ref=o_ref, vt_scr=vt_scr, qt_scr=qt_scr, z_scr=z_scr,
        acc_scr=acc_scr)

    def first_step():
        for hh in range(HB):
            _store_minus_rows(qt_scr, hh, 3 * (head0 + hh), 3)

    _attn_kernel_body(tile, v_ref, vt_scr, nblk, nq, tq_last, first_step)


def _fox_attention(qkv, c, g, batch, lp):
    spec = pl.BlockSpec((lp, LANE), lambda b, g, i: (b, 0))
    return _attn_call(_fox_kernel, "fox_attention", qkv, (c, spec), g, batch, lp, 3)


def _row_is_token(i, tm, lp, rows):
    r = i * tm + lax.broadcasted_iota(jnp.int32, (tm, 1), 0)
    is_pad = jnp.zeros((tm, 1), jnp.bool_)
    for start in range(0, rows, lp):
        is_pad = jnp.logical_or(is_pad, jnp.logical_and(r >= start, r < start + PAD))
    return jnp.logical_not(is_pad)


def _outproj_kernel(osb_ref, ofx_ref, wa_ref, wb_ref, h_ref, g_ref, out_ref, *, tm, lp, rows):
    mix = jnp.zeros((tm, D_MODEL), f32)
    for o_ref, w_ref in ((osb_ref, wa_ref), (ofx_ref, wb_ref)):
        for grp in range(o_ref.shape[0]):
            mix = mix + jnp.dot(o_ref[grp], w_ref[grp * GW:(grp + 1) * GW, :], preferred_element_type=f32)
    hn = h_ref[...] + _rms(mix, g_ref[...])
    out_ref[...] = jnp.where(_row_is_token(pl.program_id(0), tm, lp, rows), hn, 0.0)


def _outproj(o_sb, o_fx, w_a, w_b, h, g, lp, tm=512):
    rows = h.shape[0]
    ng = o_sb.shape[0]
    return pl.pallas_call(
        partial(_outproj_kernel, tm=tm, lp=lp, rows=rows),
        grid=(rows // tm,),
        in_specs=[
            pl.BlockSpec((ng, tm, GW), lambda i: (0, i, 0)),
            pl.BlockSpec((ng, tm, GW), lambda i: (0, i, 0)),
            pl.BlockSpec((W_GRP, D_MODEL), lambda i: (0, 0)),
            pl.BlockSpec((W_GRP, D_MODEL), lambda i: (0, 0)),
            pl.BlockSpec((tm, D_MODEL), lambda i: (i, 0)),
            pl.BlockSpec((1, D_MODEL), lambda i: (0, 0)),
        ],
        out_specs=pl.BlockSpec((tm, D_MODEL), lambda i: (i, 0)),
        out_shape=jax.ShapeDtypeStruct((rows, D_MODEL), f32),
        compiler_params=pltpu.CompilerParams(
            dimension_semantics=("parallel",), vmem_limit_bytes=VMEM_LIMIT),
        name="outproj",
    )(o_sb, o_fx, w_a, w_b, h, g)


def _upconv_kernel(h_ref, g_ref, wg_ref, wu_ref, cwg_ref, cwu_ref, cbg_ref, cbu_ref, out_ref,
                   u_scr, halo_scr, a_scr, *, tm):
    i = pl.program_id(0)
    j = pl.program_id(1)

    @pl.when(j == 0)
    def _():
        u_scr[...] = _rms(h_ref[...], g_ref[...]).astype(bf16)

    @pl.when(i == 0)
    def _():
        halo_scr[j] = jnp.zeros(halo_scr.shape[1:], f32)

    u = u_scr[...]

    def conv(w_ref, cw_ref, cb_ref, slot):
        scr = a_scr.at[slot]
        a = jnp.dot(u, w_ref[...], preferred_element_type=f32)
        scr[pl.ds(0, SUBLANE), :] = halo_scr[j, slot]
        scr[pl.ds(SUBLANE, tm), :] = a
        halo_scr[j, slot] = a[tm - SUBLANE:, :]
        cw = cw_ref[...]
        return (cw[0:1] * scr[pl.ds(SUBLANE - 2, tm), :] + cw[1:2] * scr[pl.ds(SUBLANE - 1, tm), :]
                + cw[2:3] * a + cb_ref[...])

    gate = conv(wg_ref, cwg_ref, cbg_ref, 0)
    up = conv(wu_ref, cwu_ref, cbu_ref, 1)
    out_ref[...] = (gate * (1.0 / (1.0 + jnp.exp(-gate))) * up).astype(bf16)


def _upconv(h, g, w_up, conv_w, conv_b, tm=512, tn=512):
    rows = h.shape[0]
    nn = D_FF // tn
    return pl.pallas_call(
        partial(_upconv_kernel, tm=tm),
        grid=(rows // tm, nn),
        in_specs=[
            pl.BlockSpec((tm, D_MODEL), lambda i, j: (i, 0)),
            pl.BlockSpec((1, D_MODEL), lambda i, j: (0, 0)),
            pl.BlockSpec((D_MODEL, tn), lambda i, j: (0, j)),
            pl.BlockSpec((D_MODEL, tn), lambda i, j: (0, nn + j)),
            pl.BlockSpec((3, tn), lambda i, j: (0, j)),
            pl.BlockSpec((3, tn), lambda i, j: (0, nn + j)),
            pl.BlockSpec((1, tn), lambda i, j: (0, j)),
            pl.BlockSpec((1, tn), lambda i, j: (0, nn + j)),
        ],
        out_specs=pl.BlockSpec((tm, tn), lambda i, j: (i, j)),
        out_shape=jax.ShapeDtypeStruct((rows, D_FF), bf16),
        scratch_shapes=[
            pltpu.VMEM((tm, D_MODEL), bf16),
            pltpu.VMEM((nn, 2, SUBLANE, tn), f32),
            pltpu.VMEM((2, tm + SUBLANE, tn), f32),
        ],
        compiler_params=pltpu.CompilerParams(
            dimension_semantics=("arbitrary", "arbitrary"), vmem_limit_bytes=VMEM_LIMIT),
        name="upconv",
    )(h, g, w_up, w_up, conv_w, conv_w, conv_b, conv_b)


def _downproj_kernel(a_ref, w_ref, h_ref, g_ref, out_ref, *, tm, lp, rows):
    ff = jnp.dot(a_ref[...], w_ref[...], preferred_element_type=f32)
    hn = h_ref[...] + _rms(ff, g_ref[...])
    out_ref[...] = jnp.where(_row_is_token(pl.program_id(0), tm, lp, rows), hn, 0.0)


def _downproj(a, w_down, h, g, lp, tm=256):
    rows = h.shape[0]
    return pl.pallas_call(
        partial(_downproj_kernel, tm=tm, lp=lp, rows=rows),
        grid=(rows // tm,),
        in_specs=[
            pl.BlockSpec((tm, D_FF), lambda i: (i, 0)),
            pl.BlockSpec((D_FF, D_MODEL), lambda i: (0, 0), pipeline_mode=pl.Buffered(1)),
            pl.BlockSpec((tm, D_MODEL), lambda i: (i, 0)),
            pl.BlockSpec((1, D_MODEL), lambda i: (0, 0)),
        ],
        out_specs=pl.BlockSpec((tm, D_MODEL), lambda i: (i, 0)),
        out_shape=jax.ShapeDtypeStruct((rows, D_MODEL), f32),
        compiler_params=pltpu.CompilerParams(
            dimension_semantics=("parallel",), vmem_limit_bytes=VMEM_LIMIT),
        name="downproj",
    )(a, w_down, h, g)


def kernel(x, meta, g_mix_pre, w_in, b_f, g_sb, g_fox, w_out, g_mix_post, g_ffn_pre, w_up, conv_w, conv_b,
           w_down, g_ffn_post):
    batch, seq, _ = x.shape
    depth = w_in.shape[0]
    lp = PAD + N_META + seq
    h = jnp.concatenate([
        jnp.zeros((batch, PAD, D_MODEL), x.dtype),
        jnp.broadcast_to(meta[None].astype(x.dtype), (batch, N_META, D_MODEL)),
        x], axis=1).reshape(batch * lp, D_MODEL)
    pad_col = jnp.zeros((lp, LANE), bf16).at[:PAD, 0].set(-NEG)
    for d in range(depth):
        w_qkv = w_in[d, :, :W_QKV].astype(bf16)
        w_f = jnp.pad(w_in[d, :, W_QKV:], ((0, 0), (0, LANE - N_HEADS))).astype(bf16)
        bias_f = jnp.pad(b_f[d], (0, LANE - N_HEADS)).reshape(1, LANE)
        qkv, lf = _inproj(h, g_mix_pre[d].reshape(1, D_MODEL), w_qkv, w_f, bias_f)
        c = _cumsum(lf, batch, lp)
        o_sb = _sb_attention(qkv, pad_col, g_sb[d].reshape(N_HEADS, 1, HEAD_DIM), batch, lp)
        o_fx = _fox_attention(qkv, c, g_fox[d].reshape(N_HEADS, 1, HEAD_DIM), batch, lp)
        w_o = w_out[d].astype(bf16)
        h = _outproj(o_sb, o_fx, w_o[:W_GRP], w_o[W_GRP:], h, g_mix_post[d].reshape(1, D_MODEL), lp)
        a = _upconv(h, g_ffn_pre[d].reshape(1, D_MODEL), w_up[d].astype(bf16), conv_w[d],
                    conv_b[d].reshape(1, 2 * D_FF))
        h = _downproj(a, w_down[d].astype(bf16), h, g_ffn_post[d].reshape(1, D_MODEL), lp)
    return h.reshape(batch, lp, D_MODEL)[:, PAD + N_META:]
```

```python
import math
from functools import partial

import jax
import jax.numpy as jnp
from jax import lax
from jax.experimental import pallas as pl
from jax.experimental.pallas import tpu as pltpu

D_MODEL = 2048
N_META = 16
HEAD_DIM = 128
N_HEADS = 8
W_GRP = N_HEADS * HEAD_DIM
W_QKV = 6 * W_GRP
D_FF = 11 * D_MODEL // 4
EPS = 1e-6
LOG2E = math.log2(math.e)
Q_SCALE = HEAD_DIM ** -0.5 * LOG2E

LANE = 128
SUBLANE = 8
BLK = 128
TQ = 256
HB = 4
GW = HB * HEAD_DIM
PAD = BLK - N_META
NEG = -1e30
VMEM_LIMIT = 58 * 1024 * 1024

f32 = jnp.float32
bf16 = jnp.bfloat16


def _rms(x, g):
    return x * lax.rsqrt(jnp.mean(x * x, axis=-1, keepdims=True) + EPS) * g


def _softplus(z):
    return jnp.maximum(z, 0.0) + jnp.log(1.0 + jnp.exp(-jnp.abs(z)))


def _softplus2(z):
    neg_abs = lax.bitcast_convert_type(
        lax.bitcast_convert_type(z, jnp.uint32) | jnp.uint32(0x80000000), f32)
    return jnp.maximum(z, 0.0) + jnp.log(1.0 + jnp.exp2(neg_abs)) * LOG2E


def _split3(x):
    h1 = x.astype(bf16)
    r = x - h1.astype(f32)
    h2 = r.astype(bf16)
    h3 = (r - h2.astype(f32)).astype(bf16)
    return h1, h2, h3


def _inproj_kernel(h_ref, g_ref, w_ref, wf_ref, bf_ref, qkv_ref, lf_ref, u_scr):
    j = pl.program_id(1)

    @pl.when(j == 0)
    def _():
        u = _rms(h_ref[...], g_ref[...]).astype(bf16)
        u_scr[...] = u
        f = jnp.dot(u, wf_ref[...], preferred_element_type=f32) + bf_ref[...]
        lf_ref[...] = -_softplus(-f)

    acc = jnp.dot(u_scr[...], w_ref[...], preferred_element_type=f32)
    is_q = jnp.logical_or(j == 0, j == 3)
    out = (acc * jnp.where(is_q, Q_SCALE, 1.0)).astype(bf16)
    for g in range(qkv_ref.shape[0]):
        qkv_ref[g] = out[:, g * GW:(g + 1) * GW]


def _inproj(h, g, w_qkv, w_f, b_f, tm=512):
    rows = h.shape[0]
    tn = W_GRP
    return pl.pallas_call(
        _inproj_kernel,
        grid=(rows // tm, W_QKV // tn),
        in_specs=[
            pl.BlockSpec((tm, D_MODEL), lambda i, j: (i, 0)),
            pl.BlockSpec((1, D_MODEL), lambda i, j: (0, 0)),
            pl.BlockSpec((D_MODEL, tn), lambda i, j: (0, j)),
            pl.BlockSpec((D_MODEL, LANE), lambda i, j: (0, 0)),
            pl.BlockSpec((1, LANE), lambda i, j: (0, 0)),
        ],
        out_specs=[
            pl.BlockSpec((tn // GW, tm, GW), lambda i, j: (j, i, 0)),
            pl.BlockSpec((tm, LANE), lambda i, j: (i, 0)),
        ],
        out_shape=[
            jax.ShapeDtypeStruct((W_QKV // GW, rows, GW), bf16),
            jax.ShapeDtypeStruct((rows, LANE), f32),
        ],
        scratch_shapes=[pltpu.VMEM((tm, D_MODEL), bf16)],
        compiler_params=pltpu.CompilerParams(
            dimension_semantics=("parallel", "arbitrary"), vmem_limit_bytes=VMEM_LIMIT),
        name="inproj",
    )(h, g, w_qkv, w_f, b_f)


def _cumsum_kernel(lf_ref, c_ref, *, nblk):
    t = lax.broadcasted_iota(jnp.int32, (BLK, BLK), 0)
    s = lax.broadcasted_iota(jnp.int32, (BLK, BLK), 1)
    tri = (s <= t).astype(bf16)
    places = [jnp.logical_and(s == 3 * t + j, t < N_HEADS).astype(bf16) for j in range(3)]

    def body(i, carry):
        off = pl.multiple_of(i * BLK, BLK)
        tot = jnp.broadcast_to(carry, (BLK, LANE))
        for piece in _split3(lf_ref[pl.ds(off, BLK), :]):
            tot = tot + jnp.dot(tri, piece, preferred_element_type=f32)
        c2 = jnp.where(t + off < PAD, -NEG, tot * LOG2E)
        out = jnp.zeros((BLK, LANE), f32)
        for piece, place in zip(_split3(c2), places):
            out = out + jnp.dot(piece, place, preferred_element_type=f32)
        c_ref[pl.ds(off, BLK), :] = out.astype(bf16)
        return tot[BLK - 1:BLK, :]

    lax.fori_loop(0, nblk, body, jnp.zeros((1, LANE), f32))


def _cumsum(lf, batch, lp):
    return pl.pallas_call(
        partial(_cumsum_kernel, nblk=lp // BLK),
        grid=(batch,),
        in_specs=[pl.BlockSpec((lp, LANE), lambda b: (b, 0))],
        out_specs=pl.BlockSpec((lp, LANE), lambda b: (b, 0)),
        out_shape=jax.ShapeDtypeStruct((batch * lp, LANE), bf16),
        compiler_params=pltpu.CompilerParams(dimension_semantics=("parallel",)),
        name="cumsum",
    )(lf)


def _later():
    s = lax.broadcasted_iota(jnp.int32, (BLK, BLK), 0)
    j = lax.broadcasted_iota(jnp.int32, (BLK, BLK), 1)
    return (j > s).astype(bf16)


def _key_query_iota(tk, tq):
    return (lax.broadcasted_iota(jnp.int32, (tk, tq), 0), lax.broadcasted_iota(jnp.int32, (tk, tq), 1))


def _head_cols():
    return [slice(hh * HEAD_DIM, (hh + 1) * HEAD_DIM) for hh in range(HB)]


def _store_vt(v_ref, vt_scr, nblk):
    def body(c, _):
        off = pl.multiple_of(c * BLK, BLK)
        for hh, cs in enumerate(_head_cols()):
            vt_scr[hh, c] = v_ref[0, pl.ds(off, BLK), cs].T
        return 0
    lax.fori_loop(0, nblk, body, 0)


def _vt_chunk(vt_scr, hh, blk0, tk):
    parts = [vt_scr[hh, blk0 + s] for s in range(tk // BLK)]
    return jnp.concatenate(parts, axis=1) if len(parts) > 1 else parts[0]


def _store_minus_rows(qt_scr, hh, first_row, n_rows):
    r = lax.broadcasted_iota(jnp.int32, (LANE, TQ), 0)
    hit = jnp.logical_and(r >= first_row, r < first_row + n_rows)
    qt_scr[hh, HEAD_DIM:, :] = jnp.where(hit, -1.0, 0.0).astype(bf16)


def _store_qt(q_ref, qt_scr, t0, tq):
    for hh, cs in enumerate(_head_cols()):
        qt_scr[hh, 0:HEAD_DIM, 0:tq] = q_ref[0, pl.ds(t0, tq), cs].T


def _scores(k_ref, x_ref, qt_scr, row0, tk, tq):
    extra = x_ref[pl.ds(row0, tk), :]
    return [jnp.dot(jnp.concatenate([k_ref[0, pl.ds(row0, tk), cs], extra], axis=1), qt_scr[hh, :, 0:tq],
                    preferred_element_type=f32)
            for hh, cs in enumerate(_head_cols())]


def _finish_head(acc_t, g_row):
    inv = lax.rsqrt(jnp.mean(acc_t * acc_t, axis=0, keepdims=True) + EPS)
    return ((acc_t * inv).T * g_row).astype(bf16)


def _attn_tile(i, tq, chunk, init, fold, finish, q_ref, k_ref, x_ref, o_ref, vt_scr, qt_scr, z_scr, acc_scr):
    nsub = TQ // BLK
    t0 = pl.multiple_of(i * TQ, TQ)
    _store_qt(q_ref, qt_scr, t0, tq)
    zs = _scores(k_ref, x_ref, qt_scr, t0, tq, tq)
    nxt = _scores(k_ref, x_ref, qt_scr, pl.multiple_of(jnp.maximum(t0 - TQ, 0), TQ), TQ, tq)
    state, _, outs = chunk(zs, [_vt_chunk(vt_scr, hh, i * nsub, tq) for hh in range(HB)], init(tq), True)
    for hh in range(HB):
        acc_scr[hh, :, 0:tq] = outs[hh]
        z_scr[hh, :, 0:tq] = nxt[hh]

    def body(n, state):
        c = i - 1 - n
        zs = [z_scr[hh, :, 0:tq] for hh in range(HB)]
        nxt = _scores(k_ref, x_ref, qt_scr, pl.multiple_of(jnp.maximum(c - 1, 0) * TQ, TQ), TQ, tq)
        state, scales, outs = chunk(zs, [_vt_chunk(vt_scr, hh, c * nsub, TQ) for hh in range(HB)], state, False)
        for hh in range(HB):
            acc_scr[hh, :, 0:tq] = fold(acc_scr[hh, :, 0:tq], scales[hh], outs[hh])
            z_scr[hh, :, 0:tq] = nxt[hh]
        return state

    state = lax.fori_loop(0, i, body, state)
    for hh, cs in enumerate(_head_cols()):
        o_ref[0, pl.ds(t0, tq), cs] = finish(acc_scr[hh, :, 0:tq], state, hh)


def _attn_kernel_body(tile, v_ref, vt_scr, nblk, nq, tq_last, first_step):
    i = pl.program_id(2)

    @pl.when(i == 0)
    def _():
        _store_vt(v_ref, vt_scr, nblk)
        first_step()

    @pl.when(i < nq - 1)
    def _():
        tile(i, TQ)

    @pl.when(i == nq - 1)
    def _():
        tile(i, tq_last)


def _sb_chunk(zs, vts, runs, diag, later):
    tk, tq = zs[0].shape
    nsub = tk // BLK
    if diag:
        kr, qc = _key_query_iota(tk, tq)
        valid = kr < qc
    ws, his, firsts = [], [], []
    for z in zs:
        sp = _softplus2(z)
        ws.append(z - sp)
        spm = jnp.where(valid, sp, 0.0) if diag else sp
        hi = spm.astype(bf16)
        his.append([hi[s * BLK:(s + 1) * BLK] for s in range(nsub)])
        firsts.append([spm[s * BLK:s * BLK + 1] for s in range(nsub)])
    withins = [[jnp.dot(later, blk, preferred_element_type=f32) for blk in hi] for hi in his]
    new_runs, outs = [], []
    for w, within, first, run, vt in zip(ws, withins, firsts, runs, vts):
        totals = [None] * nsub
        for s in reversed(range(nsub)):
            totals[s] = within[s] + run
            run = run + within[s][0:1] + first[s]
        total = jnp.concatenate(totals, axis=0) if nsub > 1 else totals[0]
        a = jnp.exp2(w - total)
        if diag:
            a = jnp.where(valid, a, 0.0)
        outs.append(jnp.dot(vt, a.astype(bf16), preferred_element_type=f32))
        new_runs.append(run)
    return new_runs, [None] * len(zs), outs


def _sb_kernel(q_ref, k_ref, v_ref, x_ref, g_ref, o_ref, vt_scr, qt_scr, z_scr, acc_scr, *, nblk, nq, tq_last):
    later = _later()
    tile = partial(
        _attn_tile,
        chunk=lambda zs, vts, runs, diag: _sb_chunk(zs, vts, runs, diag, later),
        init=lambda tq: [jnp.zeros((1, tq), f32)] * HB,
        fold=lambda acc, scale, out: acc + out,
        finish=lambda acc, runs, hh: _finish_head(acc, g_ref[hh]),
        q_ref=q_ref, k_ref=k_ref, x_ref=x_ref, o_ref=o_ref, vt_scr=vt_scr, qt_scr=qt_scr, z_scr=z_scr,
        acc_scr=acc_scr)

    def first_step():
        for hh in range(HB):
            _store_minus_rows(qt_scr, hh, 0, 1)

    _attn_kernel_body(tile, v_ref, vt_scr, nblk, nq, tq_last, first_step)


def _attn_call(kernel_fn, name, qkv, extra, g, batch, lp, first_part):
    ng = N_HEADS // HB
    nblk = lp // BLK
    nq = pl.cdiv(lp, TQ)
    part = lambda p: pl.BlockSpec((1, lp, GW), lambda b, g, i: ((first_part + p) * ng + g, b, 0))
    return pl.pallas_call(
        partial(kernel_fn, nblk=nblk, nq=nq, tq_last=lp - (nq - 1) * TQ),
        grid=(batch, ng, nq),
        in_specs=[part(0), part(1), part(2), extra[1],
                  pl.BlockSpec((HB, 1, HEAD_DIM), lambda b, g, i: (g, 0, 0))],
        out_specs=pl.BlockSpec((1, lp, GW), lambda b, g, i: (g, b, 0)),
        out_shape=jax.ShapeDtypeStruct((ng, batch * lp, GW), bf16),
        scratch_shapes=[
            pltpu.VMEM((HB, nblk, HEAD_DIM, BLK), bf16),
            pltpu.VMEM((HB, HEAD_DIM + LANE, TQ), bf16),
            pltpu.VMEM((HB, TQ, TQ), f32),
            pltpu.VMEM((HB, HEAD_DIM, TQ), f32),
        ],
        compiler_params=pltpu.CompilerParams(
            dimension_semantics=("parallel", "parallel", "arbitrary"), vmem_limit_bytes=VMEM_LIMIT),
        name=name,
    )(qkv, qkv, qkv, extra[0], g)


def _sb_attention(qkv, pad_col, g, batch, lp):
    spec = pl.BlockSpec((lp, LANE), lambda b, g, i: (0, 0))
    return _attn_call(_sb_kernel, "sb_attention", qkv, (pad_col, spec), g, batch, lp, 0)


def _fox_chunk(zs, vts, state, diag):
    tk, tq = zs[0].shape
    nh = len(zs)
    ms, ls = state[:nh], state[nh:]
    if diag:
        kr, qc = _key_query_iota(tk, tq)
        valid = kr <= qc
        zs = [jnp.where(valid, z, NEG) for z in zs]
    new_ms = [jnp.maximum(m, jnp.max(z, axis=0, keepdims=True)) for z, m in zip(zs, ms)]
    alphas = [jnp.exp2(m - m_new) for m, m_new in zip(ms, new_ms)]
    new_ls, outs = [], []
    for z, m_new, alpha, l, vt in zip(zs, new_ms, alphas, ls, vts):
        p = jnp.exp2(z - m_new)
        if diag:
            p = jnp.where(valid, p, 0.0)
        new_ls.append(alpha * l + jnp.sum(p, axis=0, keepdims=True))
        outs.append(jnp.dot(vt, p.astype(bf16), preferred_element_type=f32))
    return new_ms + new_ls, alphas, outs


def _fox_kernel(q_ref, k_ref, v_ref, x_ref, g_ref, o_ref, vt_scr, qt_scr, z_scr, acc_scr, *, nblk, nq, tq_last):
    head0 = pl.program_id(1) * HB

    def finish(acc, state, hh):
        l = state[HB + hh]
        return _finish_head(acc / jnp.where(l > 0.0, l, 1.0), g_ref[hh])

    tile = partial(
        _attn_tile,
        chunk=_fox_chunk,
        init=lambda tq: [jnp.full((1, tq), NEG, f32)] * HB + [jnp.zeros((1, tq), f32)] * HB,
        fold=lambda acc, scale, out: scale * acc + out,
        finish=finish,
        q_ref=q_ref, k_ref=k_ref, x_ref=x_ref, o_ref=o_ref, vt_scr=vt_scr, qt_scr=qt_scr, z_scr=z_scr,
        acc_scr=acc_scr)

    def first_step():
        for hh in range(HB):
            _store_minus_rows(qt_scr, hh, 3 * (head0 + hh), 3)

    _attn_kernel_body(tile, v_ref, vt_scr, nblk, nq, tq_last, first_step)


def _fox_attention(qkv, c, g, batch, lp):
    spec = pl.BlockSpec((lp, LANE), lambda b, g, i: (b, 0))
    return _attn_call(_fox_kernel, "fox_attention", qkv, (c, spec), g, batch, lp, 3)


def _row_is_token(i, tm, lp, rows):
    r = i * tm + lax.broadcasted_iota(jnp.int32, (tm, 1), 0)
    is_pad = jnp.zeros((tm, 1), jnp.bool_)
    for start in range(0, rows, lp):
        is_pad = jnp.logical_or(is_pad, jnp.logical_and(r >= start, r < start + PAD))
    return jnp.logical_not(is_pad)


def _outproj_kernel(osb_ref, ofx_ref, wa_ref, wb_ref, h_ref, g_ref, out_ref, *, tm, lp, rows):
    mix = jnp.zeros((tm, D_MODEL), f32)
    for o_ref, w_ref in ((osb_ref, wa_ref), (ofx_ref, wb_ref)):
        for grp in range(o_ref.shape[0]):
            mix = mix + jnp.dot(o_ref[grp], w_ref[grp * GW:(grp + 1) * GW, :], preferred_element_type=f32)
    hn = h_ref[...] + _rms(mix, g_ref[...])
    out_ref[...] = jnp.where(_row_is_token(pl.program_id(0), tm, lp, rows), hn, 0.0)


def _outproj(o_sb, o_fx, w_o, h, g, lp, tm=512):
    rows = h.shape[0]
    ng = o_sb.shape[0]
    return pl.pallas_call(
        partial(_outproj_kernel, tm=tm, lp=lp, rows=rows),
        grid=(rows // tm,),
        in_specs=[
            pl.BlockSpec((ng, tm, GW), lambda i: (0, i, 0)),
            pl.BlockSpec((ng, tm, GW), lambda i: (0, i, 0)),
            pl.BlockSpec((W_GRP, D_MODEL), lambda i: (0, 0)),
            pl.BlockSpec((W_GRP, D_MODEL), lambda i: (1, 0)),
            pl.BlockSpec((tm, D_MODEL), lambda i: (i, 0)),
            pl.BlockSpec((1, D_MODEL), lambda i: (0, 0)),
        ],
        out_specs=pl.BlockSpec((tm, D_MODEL), lambda i: (i, 0)),
        out_shape=jax.ShapeDtypeStruct((rows, D_MODEL), f32),
        compiler_params=pltpu.CompilerParams(
            dimension_semantics=("parallel",), vmem_limit_bytes=VMEM_LIMIT),
        name="outproj",
    )(o_sb, o_fx, w_o, w_o, h, g)


def _upconv_kernel(h_ref, g_ref, wg_ref, wu_ref, cwg_ref, cwu_ref, cbg_ref, cbu_ref, out_ref,
                   u_scr, halo_scr, a_scr, *, tm):
    i = pl.program_id(0)
    j = pl.program_id(1)

    @pl.when(j == 0)
    def _():
        u_scr[...] = _rms(h_ref[...], g_ref[...]).astype(bf16)

    @pl.when(i == 0)
    def _():
        halo_scr[j] = jnp.zeros(halo_scr.shape[1:], f32)

    u = u_scr[...]

    def conv(w_ref, cw_ref, cb_ref, slot):
        scr = a_scr.at[slot]
        a = jnp.dot(u, w_ref[...], preferred_element_type=f32)
        scr[pl.ds(0, SUBLANE), :] = halo_scr[j, slot]
        scr[pl.ds(SUBLANE, tm), :] = a
        halo_scr[j, slot] = a[tm - SUBLANE:, :]
        cw = cw_ref[...]
        return (cw[0:1] * scr[pl.ds(SUBLANE - 2, tm), :] + cw[1:2] * scr[pl.ds(SUBLANE - 1, tm), :]
                + cw[2:3] * a + cb_ref[...])

    gate = conv(wg_ref, cwg_ref, cbg_ref, 0)
    up = conv(wu_ref, cwu_ref, cbu_ref, 1)
    out_ref[...] = (gate * (1.0 / (1.0 + jnp.exp(-gate))) * up).astype(bf16)


def _upconv(h, g, w_up, conv_w, conv_b, tm=512, tn=512):
    rows = h.shape[0]
    nn = D_FF // tn
    return pl.pallas_call(
        partial(_upconv_kernel, tm=tm),
        grid=(rows // tm, nn),
        in_specs=[
            pl.BlockSpec((tm, D_MODEL), lambda i, j: (i, 0)),
            pl.BlockSpec((1, D_MODEL), lambda i, j: (0, 0)),
            pl.BlockSpec((D_MODEL, tn), lambda i, j: (0, j)),
            pl.BlockSpec((D_MODEL, tn), lambda i, j: (0, nn + j)),
            pl.BlockSpec((3, tn), lambda i, j: (0, j)),
            pl.BlockSpec((3, tn), lambda i, j: (0, nn + j)),
            pl.BlockSpec((1, tn), lambda i, j: (0, j)),
            pl.BlockSpec((1, tn), lambda i, j: (0, nn + j)),
        ],
        out_specs=pl.BlockSpec((tm, tn), lambda i, j: (i, j)),
        out_shape=jax.ShapeDtypeStruct((rows, D_FF), bf16),
        scratch_shapes=[
            pltpu.VMEM((tm, D_MODEL), bf16),
            pltpu.VMEM((nn, 2, SUBLANE, tn), f32),
            pltpu.VMEM((2, tm + SUBLANE, tn), f32),
        ],
        compiler_params=pltpu.CompilerParams(
            dimension_semantics=("arbitrary", "arbitrary"), vmem_limit_bytes=VMEM_LIMIT),
        name="upconv",
    )(h, g, w_up, w_up, conv_w, conv_w, conv_b, conv_b)


def _downproj_kernel(*refs, nparts, token_rows):
    a_refs, w_ref, h_refs = refs[:nparts], refs[nparts], refs[nparts + 1:2 * nparts + 1]
    g_ref, out_ref = refs[-2:]
    rows_of = lambda parts: parts[0][...] if nparts == 1 else jnp.concatenate([p[...] for p in parts], axis=0)
    ff = jnp.dot(rows_of(a_refs), w_ref[...], preferred_element_type=f32)
    hn = rows_of(h_refs) + _rms(ff, g_ref[...])
    out_ref[...] = hn if token_rows is None else jnp.where(token_rows(pl.program_id(0)), hn, 0.0)


def _downproj_call(a_specs, h_specs, out_spec, out_rows, steps, token_rows, a, w_down, h, g):
    nparts = len(a_specs)
    return pl.pallas_call(
        partial(_downproj_kernel, nparts=nparts, token_rows=token_rows),
        grid=(steps,),
        in_specs=a_specs + [pl.BlockSpec((D_FF, D_MODEL), lambda i: (0, 0), pipeline_mode=pl.Buffered(1))]
        + h_specs + [pl.BlockSpec((1, D_MODEL), lambda i: (0, 0))],
        out_specs=out_spec,
        out_shape=jax.ShapeDtypeStruct((out_rows, D_MODEL), f32),
        compiler_params=pltpu.CompilerParams(
            dimension_semantics=("parallel",), vmem_limit_bytes=VMEM_LIMIT),
        name="downproj",
    )(*([a] * nparts), w_down, *([h] * nparts), g)


def _downproj(a, w_down, h, g, lp, tm=256):
    rows = h.shape[0]
    return _downproj_call(
        [pl.BlockSpec((tm, D_FF), lambda i: (i, 0))], [pl.BlockSpec((tm, D_MODEL), lambda i: (i, 0))],
        pl.BlockSpec((tm, D_MODEL), lambda i: (i, 0)), rows, rows // tm,
        partial(_row_is_token, tm=tm, lp=lp, rows=rows), a, w_down, h, g)


def _downproj_tokens(a, w_down, h, g, batch, lp, seq, tm=256):
    nparts = tm // BLK
    per_batch = seq // tm
    first = lambda i, p: (i // per_batch) * (lp // BLK) + 1 + (i % per_batch) * nparts + p
    return _downproj_call(
        [pl.BlockSpec((BLK, D_FF), lambda i, p=p: (first(i, p), 0)) for p in range(nparts)],
        [pl.BlockSpec((BLK, D_MODEL), lambda i, p=p: (first(i, p), 0)) for p in range(nparts)],
        pl.BlockSpec((tm, D_MODEL), lambda i: (i, 0)), batch * seq, batch * per_batch, None, a, w_down, h, g)


def kernel(x, meta, g_mix_pre, w_in, b_f, g_sb, g_fox, w_out, g_mix_post, g_ffn_pre, w_up, conv_w, conv_b,
           w_down, g_ffn_post):
    batch, seq, _ = x.shape
    depth = w_in.shape[0]
    lp = PAD + N_META + seq
    h = jnp.concatenate([
        jnp.zeros((batch, PAD, D_MODEL), x.dtype),
        jnp.broadcast_to(meta[None].astype(x.dtype), (batch, N_META, D_MODEL)),
        x], axis=1).reshape(batch * lp, D_MODEL)
    pad_col = jnp.zeros((lp, LANE), bf16).at[:PAD, 0].set(-NEG)
    for d in range(depth):
        w_i = w_in[d].astype(bf16)
        w_f = jnp.pad(w_in[d, :, W_QKV:], ((0, 0), (0, LANE - N_HEADS))).astype(bf16)
        bias_f = jnp.pad(b_f[d], (0, LANE - N_HEADS)).reshape(1, LANE)
        qkv, lf = _inproj(h, g_mix_pre[d].reshape(1, D_MODEL), w_i, w_f, bias_f)
        c = _cumsum(lf, batch, lp)
        o_sb = _sb_attention(qkv, pad_col, g_sb[d].reshape(N_HEADS, 1, HEAD_DIM), batch, lp)
        o_fx = _fox_attention(qkv, c, g_fox[d].reshape(N_HEADS, 1, HEAD_DIM), batch, lp)
        h = _outproj(o_sb, o_fx, w_out[d].astype(bf16), h, g_mix_post[d].reshape(1, D_MODEL), lp)
        a = _upconv(h, g_ffn_pre[d].reshape(1, D_MODEL), w_up[d].astype(bf16), conv_w[d],
                    conv_b[d].reshape(1, 2 * D_FF))
        g_post = g_ffn_post[d].reshape(1, D_MODEL)
        if d + 1 < depth:
            h = _downproj(a, w_down[d].astype(bf16), h, g_post, lp)
        else:
            h = _downproj_tokens(a, w_down[d].astype(bf16), h, g_post, batch, lp, seq)
    return h.reshape(batch, seq, D_MODEL)
```

```python
import math
from functools import partial

import jax
import jax.numpy as jnp
from jax import lax
from jax.experimental import pallas as pl
from jax.experimental.pallas import tpu as pltpu

D_MODEL = 2048
N_META = 16
HEAD_DIM = 128
N_HEADS = 8
W_GRP = N_HEADS * HEAD_DIM
W_QKV = 6 * W_GRP
D_FF = 11 * D_MODEL // 4
EPS = 1e-6
LOG2E = math.log2(math.e)
Q_SCALE = HEAD_DIM ** -0.5 * LOG2E

LANE = 128
SUBLANE = 8
BLK = 128
TQ = 256
HB = 4
GW = HB * HEAD_DIM
PAD = BLK - N_META
NEG = -1e30
VMEM_LIMIT = 58 * 1024 * 1024

f32 = jnp.float32
bf16 = jnp.bfloat16


def _rms(x, g):
    return x * lax.rsqrt(jnp.mean(x * x, axis=-1, keepdims=True) + EPS) * g


def _softplus(z):
    return jnp.maximum(z, 0.0) + jnp.log(1.0 + jnp.exp(-jnp.abs(z)))


def _softplus2(z):
    neg_abs = lax.bitcast_convert_type(
        lax.bitcast_convert_type(z, jnp.uint32) | jnp.uint32(0x80000000), f32)
    return jnp.maximum(z, 0.0) + jnp.log(1.0 + jnp.exp2(neg_abs)) * LOG2E


def _split3(x):
    h1 = x.astype(bf16)
    r = x - h1.astype(f32)
    h2 = r.astype(bf16)
    h3 = (r - h2.astype(f32)).astype(bf16)
    return h1, h2, h3


def _inproj_kernel(h_ref, g_ref, w_ref, wf_ref, bf_ref, qkv_ref, lf_ref, u_scr):
    j = pl.program_id(1)

    @pl.when(j == 0)
    def _():
        u = _rms(h_ref[...], g_ref[...]).astype(bf16)
        u_scr[...] = u
        f = jnp.dot(u, wf_ref[...], preferred_element_type=f32) + bf_ref[...]
        lf_ref[...] = -_softplus(-f)

    acc = jnp.dot(u_scr[...], w_ref[...], preferred_element_type=f32)
    is_q = jnp.logical_or(j == 0, j == 3)
    out = (acc * jnp.where(is_q, Q_SCALE, 1.0)).astype(bf16)
    for g in range(qkv_ref.shape[0]):
        qkv_ref[g] = out[:, g * GW:(g + 1) * GW]


def _inproj(h, g, w_in, layer, w_f, b_f, tm=512):
    rows = h.shape[0]
    tn = W_GRP
    return pl.pallas_call(
        _inproj_kernel,
        grid=(rows // tm, W_QKV // tn),
        in_specs=[
            pl.BlockSpec((tm, D_MODEL), lambda i, j: (i, 0)),
            pl.BlockSpec((1, D_MODEL), lambda i, j: (0, 0)),
            pl.BlockSpec((None, D_MODEL, tn), lambda i, j: (layer, 0, j)),
            pl.BlockSpec((D_MODEL, LANE), lambda i, j: (0, 0)),
            pl.BlockSpec((1, LANE), lambda i, j: (0, 0)),
        ],
        out_specs=[
            pl.BlockSpec((tn // GW, tm, GW), lambda i, j: (j, i, 0)),
            pl.BlockSpec((tm, LANE), lambda i, j: (i, 0)),
        ],
        out_shape=[
            jax.ShapeDtypeStruct((W_QKV // GW, rows, GW), bf16),
            jax.ShapeDtypeStruct((rows, LANE), f32),
        ],
        scratch_shapes=[pltpu.VMEM((tm, D_MODEL), bf16)],
        compiler_params=pltpu.CompilerParams(
            dimension_semantics=("parallel", "arbitrary"), vmem_limit_bytes=VMEM_LIMIT),
        name="inproj",
    )(h, g, w_in, w_f, b_f)


def _cumsum_kernel(lf_ref, c_ref, *, batch, lp):
    t = lax.broadcasted_iota(jnp.int32, (BLK, BLK), 0)
    s = lax.broadcasted_iota(jnp.int32, (BLK, BLK), 1)
    tri = (s <= t).astype(bf16)
    places = [jnp.logical_and(s == 3 * t + j, t < N_HEADS).astype(bf16) for j in range(3)]

    def body(i, carries):
        new = []
        for b, carry in enumerate(carries):
            off = pl.multiple_of(b * lp + i * BLK, BLK)
            tot = jnp.broadcast_to(carry, (BLK, LANE))
            for piece in _split3(lf_ref[pl.ds(off, BLK), :]):
                tot = tot + jnp.dot(tri, piece, preferred_element_type=f32)
            c2 = jnp.where(t + i * BLK < PAD, -NEG, tot * LOG2E)
            out = jnp.zeros((BLK, LANE), f32)
            for piece, place in zip(_split3(c2), places):
                out = out + jnp.dot(piece, place, preferred_element_type=f32)
            c_ref[pl.ds(off, BLK), :] = out.astype(bf16)
            new.append(tot[BLK - 1:BLK, :])
        return tuple(new)

    lax.fori_loop(0, lp // BLK, body, (jnp.zeros((1, LANE), f32),) * batch)


def _cumsum(lf, batch, lp):
    rows = batch * lp
    return pl.pallas_call(
        partial(_cumsum_kernel, batch=batch, lp=lp),
        grid=(1,),
        in_specs=[pl.BlockSpec((rows, LANE), lambda i: (0, 0))],
        out_specs=pl.BlockSpec((rows, LANE), lambda i: (0, 0)),
        out_shape=jax.ShapeDtypeStruct((rows, LANE), bf16),
        compiler_params=pltpu.CompilerParams(
            dimension_semantics=("arbitrary",), vmem_limit_bytes=VMEM_LIMIT),
        name="cumsum",
    )(lf)


def _later():
    s = lax.broadcasted_iota(jnp.int32, (BLK, BLK), 0)
    j = lax.broadcasted_iota(jnp.int32, (BLK, BLK), 1)
    return (j > s).astype(bf16)


def _key_query_iota(tk, tq):
    return (lax.broadcasted_iota(jnp.int32, (tk, tq), 0), lax.broadcasted_iota(jnp.int32, (tk, tq), 1))


def _head_cols():
    return [slice(hh * HEAD_DIM, (hh + 1) * HEAD_DIM) for hh in range(HB)]


def _store_vt(v_ref, vt_scr, nblk):
    def body(c, _):
        off = pl.multiple_of(c * BLK, BLK)
        for hh, cs in enumerate(_head_cols()):
            vt_scr[hh, c] = v_ref[0, pl.ds(off, BLK), cs].astype(f32).T.astype(bf16)
        return 0
    lax.fori_loop(0, nblk, body, 0)


def _vt_chunk(vt_scr, hh, blk0, tk):
    parts = [vt_scr[hh, blk0 + s] for s in range(tk // BLK)]
    return jnp.concatenate(parts, axis=1) if len(parts) > 1 else parts[0]


def _store_minus_rows(qt_scr, hh, first_row, n_rows):
    r = lax.broadcasted_iota(jnp.int32, (LANE, TQ), 0)
    hit = jnp.logical_and(r >= first_row, r < first_row + n_rows)
    qt_scr[hh, HEAD_DIM:, :] = jnp.where(hit, -1.0, 0.0).astype(bf16)


def _store_qt(q_ref, qt_scr, t0, tq):
    for hh, cs in enumerate(_head_cols()):
        qt_scr[hh, 0:HEAD_DIM, 0:tq] = q_ref[0, pl.ds(t0, tq), cs].astype(f32).T.astype(bf16)


def _scores(k_ref, x_ref, qt_scr, row0, tk, tq):
    extra = x_ref[pl.ds(row0, tk), :]
    return [jnp.dot(jnp.concatenate([k_ref[0, pl.ds(row0, tk), cs], extra], axis=1), qt_scr[hh, :, 0:tq],
                    preferred_element_type=f32)
            for hh, cs in enumerate(_head_cols())]


def _finish_head(acc_t, g_row):
    inv = lax.rsqrt(jnp.mean(acc_t * acc_t, axis=0, keepdims=True) + EPS)
    return ((acc_t * inv).T * g_row).astype(bf16)


def _attn_tile(i, tq, chunk, init, fold, finish, q_ref, k_ref, x_ref, o_ref, vt_scr, qt_scr, z_scr, acc_scr):
    nsub = TQ // BLK
    t0 = pl.multiple_of(i * TQ, TQ)
    _store_qt(q_ref, qt_scr, t0, tq)
    zs = _scores(k_ref, x_ref, qt_scr, t0, tq, tq)
    nxt = _scores(k_ref, x_ref, qt_scr, pl.multiple_of(jnp.maximum(t0 - TQ, 0), TQ), TQ, tq)
    state, _, outs = chunk(zs, [_vt_chunk(vt_scr, hh, i * nsub, tq) for hh in range(HB)], init(tq), True)
    for hh in range(HB):
        acc_scr[hh, :, 0:tq] = outs[hh]
        z_scr[hh, :, 0:tq] = nxt[hh]

    def body(n, state):
        c = i - 1 - n
        zs = [z_scr[hh, :, 0:tq] for hh in range(HB)]
        nxt = _scores(k_ref, x_ref, qt_scr, pl.multiple_of(jnp.maximum(c - 1, 0) * TQ, TQ), TQ, tq)
        state, scales, outs = chunk(zs, [_vt_chunk(vt_scr, hh, c * nsub, TQ) for hh in range(HB)], state, False)
        for hh in range(HB):
            acc_scr[hh, :, 0:tq] = fold(acc_scr[hh, :, 0:tq], scales[hh], outs[hh])
            z_scr[hh, :, 0:tq] = nxt[hh]
        return state

    state = lax.fori_loop(0, i, body, state)
    for hh, cs in enumerate(_head_cols()):
        o_ref[0, pl.ds(t0, tq), cs] = finish(acc_scr[hh, :, 0:tq], state, hh)


def _attn_kernel_body(tile, v_ref, vt_scr, nblk, nq, tq_last, first_step):
    i = pl.program_id(2)

    @pl.when(i == 0)
    def _():
        _store_vt(v_ref, vt_scr, nblk)
        first_step()

    @pl.when(i < nq - 1)
    def _():
        tile(i, TQ)

    @pl.when(i == nq - 1)
    def _():
        tile(i, tq_last)


def _sb_chunk(zs, vts, runs, diag, later):
    tk, tq = zs[0].shape
    nsub = tk // BLK
    if diag:
        kr, qc = _key_query_iota(tk, tq)
        valid = kr < qc
    ws, his, firsts = [], [], []
    for z in zs:
        sp = _softplus2(z)
        ws.append(z - sp)
        spm = jnp.where(valid, sp, 0.0) if diag else sp
        hi = spm.astype(bf16)
        his.append([hi[s * BLK:(s + 1) * BLK] for s in range(nsub)])
        firsts.append([spm[s * BLK:s * BLK + 1] for s in range(nsub)])
    withins = [[jnp.dot(later, blk, preferred_element_type=f32) for blk in hi] for hi in his]
    new_runs, outs = [], []
    for w, within, first, run, vt in zip(ws, withins, firsts, runs, vts):
        totals = [None] * nsub
        for s in reversed(range(nsub)):
            totals[s] = within[s] + run
            run = run + within[s][0:1] + first[s]
        total = jnp.concatenate(totals, axis=0) if nsub > 1 else totals[0]
        a = jnp.exp2(w - total)
        if diag:
            a = jnp.where(valid, a, 0.0)
        outs.append(jnp.dot(vt, a.astype(bf16), preferred_element_type=f32))
        new_runs.append(run)
    return new_runs, [None] * len(zs), outs


def _sb_kernel(q_ref, k_ref, v_ref, x_ref, g_ref, o_ref, vt_scr, qt_scr, z_scr, acc_scr, *, nblk, nq, tq_last):
    later = _later()
    tile = partial(
        _attn_tile,
        chunk=lambda zs, vts, runs, diag: _sb_chunk(zs, vts, runs, diag, later),
        init=lambda tq: [jnp.zeros((1, tq), f32)] * HB,
        fold=lambda acc, scale, out: acc + out,
        finish=lambda acc, runs, hh: _finish_head(acc, g_ref[hh]),
        q_ref=q_ref, k_ref=k_ref, x_ref=x_ref, o_ref=o_ref, vt_scr=vt_scr, qt_scr=qt_scr, z_scr=z_scr,
        acc_scr=acc_scr)

    def first_step():
        for hh in range(HB):
            _store_minus_rows(qt_scr, hh, 0, 1)

    _attn_kernel_body(tile, v_ref, vt_scr, nblk, nq, tq_last, first_step)


def _attn_call(kernel_fn, name, qkv, extra, g, batch, lp, first_part):
    ng = N_HEADS // HB
    nblk = lp // BLK
    nq = pl.cdiv(lp, TQ)
    part = lambda p: pl.BlockSpec((1, lp, GW), lambda b, g, i: ((first_part + p) * ng + g, b, 0))
    return pl.pallas_call(
        partial(kernel_fn, nblk=nblk, nq=nq, tq_last=lp - (nq - 1) * TQ),
        grid=(batch, ng, nq),
        in_specs=[part(0), part(1), part(2), extra[1],
                  pl.BlockSpec((HB, 1, HEAD_DIM), lambda b, g, i: (g, 0, 0))],
        out_specs=pl.BlockSpec((1, lp, GW), lambda b, g, i: (g, b, 0)),
        out_shape=jax.ShapeDtypeStruct((ng, batch * lp, GW), bf16),
        scratch_shapes=[
            pltpu.VMEM((HB, nblk, HEAD_DIM, BLK), bf16),
            pltpu.VMEM((HB, HEAD_DIM + LANE, TQ), bf16),
            pltpu.VMEM((HB, TQ, TQ), f32),
            pltpu.VMEM((HB, HEAD_DIM, TQ), f32),
        ],
        compiler_params=pltpu.CompilerParams(
            dimension_semantics=("parallel", "parallel", "arbitrary"), vmem_limit_bytes=VMEM_LIMIT),
        name=name,
    )(qkv, qkv, qkv, extra[0], g)


def _sb_attention(qkv, pad_col, g, batch, lp):
    spec = pl.BlockSpec((lp, LANE), lambda b, g, i: (0, 0))
    return _attn_call(_sb_kernel, "sb_attention", qkv, (pad_col, spec), g, batch, lp, 0)


def _fox_chunk(zs, vts, state, diag):
    tk, tq = zs[0].shape
    nh = len(zs)
    ms, ls = state[:nh], state[nh:]
    if diag:
        kr, qc = _key_query_iota(tk, tq)
        valid = kr <= qc
        zs = [jnp.where(valid, z, NEG) for z in zs]
    new_ms = [jnp.maximum(m, jnp.max(z, axis=0, keepdims=True)) for z, m in zip(zs, ms)]
    alphas = [jnp.exp2(m - m_new) for m, m_new in zip(ms, new_ms)]
    new_ls, outs = [], []
    for z, m_new, alpha, l, vt in zip(zs, new_ms, alphas, ls, vts):
        p = jnp.exp2(z - m_new)
        if diag:
            p = jnp.where(valid, p, 0.0)
        new_ls.append(alpha * l + jnp.sum(p, axis=0, keepdims=True))
        outs.append(jnp.dot(vt, p.astype(bf16), preferred_element_type=f32))
    return new_ms + new_ls, alphas, outs


def _fox_kernel(q_ref, k_ref, v_ref, x_ref, g_ref, o_ref, vt_scr, qt_scr, z_scr, acc_scr, *, nblk, nq, tq_last):
    head0 = pl.program_id(1) * HB

    def finish(acc, state, hh):
        l = state[HB + hh]
        return _finish_head(acc / jnp.where(l > 0.0, l, 1.0), g_ref[hh])

    tile = partial(
        _attn_tile,
        chunk=_fox_chunk,
        init=lambda tq: [jnp.full((1, tq), NEG, f32)] * HB + [jnp.zeros((1, tq), f32)] * HB,
        fold=lambda acc, scale, out: scale * acc + out,
        finish=finish,
        q_ref=q_ref, k_ref=k_ref, x_ref=x_ref, o_ref=o_ref, vt_scr=vt_scr, qt_scr=qt_scr, z_scr=z_scr,
        acc_scr=acc_scr)

    def first_step():
        for hh in range(HB):
            _store_minus_rows(qt_scr, hh, 3 * (head0 + hh), 3)

    _attn_kernel_body(tile, v_ref, vt_scr, nblk, nq, tq_last, first_step)


def _fox_attention(qkv, c, g, batch, lp):
    spec = pl.BlockSpec((lp, LANE), lambda b, g, i: (b, 0))
    return _attn_call(_fox_kernel, "fox_attention", qkv, (c, spec), g, batch, lp, 3)


def _row_is_token(i, tm, lp, rows):
    r = i * tm + lax.broadcasted_iota(jnp.int32, (tm, 1), 0)
    is_pad = jnp.zeros((tm, 1), jnp.bool_)
    for start in range(0, rows, lp):
        is_pad = jnp.logical_or(is_pad, jnp.logical_and(r >= start, r < start + PAD))
    return jnp.logical_not(is_pad)


def _outproj_kernel(osb_ref, ofx_ref, wa_ref, wb_ref, h_ref, g_ref, out_ref, *, tm, lp, rows):
    mix = jnp.zeros((tm, D_MODEL), f32)
    for o_ref, w_ref in ((osb_ref, wa_ref), (ofx_ref, wb_ref)):
        for grp in range(o_ref.shape[0]):
            mix = mix + jnp.dot(o_ref[grp], w_ref[grp * GW:(grp + 1) * GW, :], preferred_element_type=f32)
    hn = h_ref[...] + _rms(mix, g_ref[...])
    out_ref[...] = jnp.where(_row_is_token(pl.program_id(0), tm, lp, rows), hn, 0.0)


def _outproj(o_sb, o_fx, w_o, layer, h, g, lp, tm=512):
    rows = h.shape[0]
    ng = o_sb.shape[0]
    return pl.pallas_call(
        partial(_outproj_kernel, tm=tm, lp=lp, rows=rows),
        grid=(rows // tm,),
        in_specs=[
            pl.BlockSpec((ng, tm, GW), lambda i: (0, i, 0)),
            pl.BlockSpec((ng, tm, GW), lambda i: (0, i, 0)),
            pl.BlockSpec((None, W_GRP, D_MODEL), lambda i: (layer, 0, 0)),
            pl.BlockSpec((None, W_GRP, D_MODEL), lambda i: (layer, 1, 0)),
            pl.BlockSpec((tm, D_MODEL), lambda i: (i, 0)),
            pl.BlockSpec((1, D_MODEL), lambda i: (0, 0)),
        ],
        out_specs=pl.BlockSpec((tm, D_MODEL), lambda i: (i, 0)),
        out_shape=jax.ShapeDtypeStruct((rows, D_MODEL), f32),
        compiler_params=pltpu.CompilerParams(
            dimension_semantics=("parallel",), vmem_limit_bytes=VMEM_LIMIT),
        name="outproj",
    )(o_sb, o_fx, w_o, w_o, h, g)


def _upconv_kernel(h_ref, g_ref, wg_ref, wu_ref, cwg_ref, cwu_ref, cbg_ref, cbu_ref, out_ref,
                   u_scr, halo_scr, a_scr, *, tm):
    i = pl.program_id(0)
    j = pl.program_id(1)

    @pl.when(j == 0)
    def _():
        u_scr[...] = _rms(h_ref[...], g_ref[...]).astype(bf16)

    @pl.when(i == 0)
    def _():
        halo_scr[j] = jnp.zeros(halo_scr.shape[1:], f32)

    u = u_scr[...]

    def conv(w_ref, cw_ref, cb_ref, slot):
        scr = a_scr.at[slot]
        a = jnp.dot(u, w_ref[...], preferred_element_type=f32)
        scr[pl.ds(0, SUBLANE), :] = halo_scr[j, slot]
        scr[pl.ds(SUBLANE, tm), :] = a
        halo_scr[j, slot] = a[tm - SUBLANE:, :]
        cw = cw_ref[...]
        return (cw[0:1] * scr[pl.ds(SUBLANE - 2, tm), :] + cw[1:2] * scr[pl.ds(SUBLANE - 1, tm), :]
                + cw[2:3] * a + cb_ref[...])

    gate = conv(wg_ref, cwg_ref, cbg_ref, 0)
    up = conv(wu_ref, cwu_ref, cbu_ref, 1)
    out_ref[...] = (gate * (1.0 / (1.0 + jnp.exp(-gate))) * up).astype(bf16)


def _upconv(h, g, w_up, layer, conv_w, conv_b, tm=512, tn=512):
    rows = h.shape[0]
    nn = D_FF // tn
    return pl.pallas_call(
        partial(_upconv_kernel, tm=tm),
        grid=(rows // tm, nn),
        in_specs=[
            pl.BlockSpec((tm, D_MODEL), lambda i, j: (i, 0)),
            pl.BlockSpec((1, D_MODEL), lambda i, j: (0, 0)),
            pl.BlockSpec((None, D_MODEL, tn), lambda i, j: (layer, 0, j)),
            pl.BlockSpec((None, D_MODEL, tn), lambda i, j: (layer, 0, nn + j)),
            pl.BlockSpec((3, tn), lambda i, j: (0, j)),
            pl.BlockSpec((3, tn), lambda i, j: (0, nn + j)),
            pl.BlockSpec((1, tn), lambda i, j: (0, j)),
            pl.BlockSpec((1, tn), lambda i, j: (0, nn + j)),
        ],
        out_specs=pl.BlockSpec((tm, tn), lambda i, j: (i, j)),
        out_shape=jax.ShapeDtypeStruct((rows, D_FF), bf16),
        scratch_shapes=[
            pltpu.VMEM((tm, D_MODEL), bf16),
            pltpu.VMEM((nn, 2, SUBLANE, tn), f32),
            pltpu.VMEM((2, tm + SUBLANE, tn), f32),
        ],
        compiler_params=pltpu.CompilerParams(
            dimension_semantics=("arbitrary", "arbitrary"), vmem_limit_bytes=VMEM_LIMIT),
        name="upconv",
    )(h, g, w_up, w_up, conv_w, conv_w, conv_b, conv_b)


def _downproj_kernel(*refs, nparts, token_rows):
    a_refs, w_ref, h_refs = refs[:nparts], refs[nparts], refs[nparts + 1:2 * nparts + 1]
    g_ref, out_ref = refs[-2:]
    rows_of = lambda parts: parts[0][...] if nparts == 1 else jnp.concatenate([p[...] for p in parts], axis=0)
    ff = jnp.dot(rows_of(a_refs), w_ref[...], preferred_element_type=f32)
    hn = rows_of(h_refs) + _rms(ff, g_ref[...])
    out_ref[...] = hn if token_rows is None else jnp.where(token_rows(pl.program_id(0)), hn, 0.0)


def _downproj_call(a_specs, h_specs, out_spec, out_rows, steps, token_rows, a, w_down, layer, h, g):
    nparts = len(a_specs)
    w_spec = pl.BlockSpec((None, D_FF, D_MODEL), lambda i: (layer, 0, 0), pipeline_mode=pl.Buffered(1))
    return pl.pallas_call(
        partial(_downproj_kernel, nparts=nparts, token_rows=token_rows),
        grid=(steps,),
        in_specs=a_specs + [w_spec] + h_specs + [pl.BlockSpec((1, D_MODEL), lambda i: (0, 0))],
        out_specs=out_spec,
        out_shape=jax.ShapeDtypeStruct((out_rows, D_MODEL), f32),
        compiler_params=pltpu.CompilerParams(
            dimension_semantics=("parallel",), vmem_limit_bytes=VMEM_LIMIT),
        name="downproj",
    )(*([a] * nparts), w_down, *([h] * nparts), g)


def _downproj(a, w_down, layer, h, g, lp, tm=256):
    rows = h.shape[0]
    return _downproj_call(
        [pl.BlockSpec((tm, D_FF), lambda i: (i, 0))], [pl.BlockSpec((tm, D_MODEL), lambda i: (i, 0))],
        pl.BlockSpec((tm, D_MODEL), lambda i: (i, 0)), rows, rows // tm,
        partial(_row_is_token, tm=tm, lp=lp, rows=rows), a, w_down, layer, h, g)


def _downproj_tokens(a, w_down, layer, h, g, batch, lp, seq, tm=256):
    nparts = tm // BLK
    per_batch = seq // tm
    first = lambda i, p: (i // per_batch) * (lp // BLK) + 1 + (i % per_batch) * nparts + p
    return _downproj_call(
        [pl.BlockSpec((BLK, D_FF), lambda i, p=p: (first(i, p), 0)) for p in range(nparts)],
        [pl.BlockSpec((BLK, D_MODEL), lambda i, p=p: (first(i, p), 0)) for p in range(nparts)],
        pl.BlockSpec((tm, D_MODEL), lambda i: (i, 0)), batch * seq, batch * per_batch, None, a, w_down, layer,
        h, g)


def kernel(x, meta, g_mix_pre, w_in, b_f, g_sb, g_fox, w_out, g_mix_post, g_ffn_pre, w_up, conv_w, conv_b,
           w_down, g_ffn_post):
    batch, seq, _ = x.shape
    depth = w_in.shape[0]
    lp = PAD + N_META + seq
    h = jnp.concatenate([
        jnp.zeros((batch, PAD, D_MODEL), x.dtype),
        jnp.broadcast_to(meta[None].astype(x.dtype), (batch, N_META, D_MODEL)),
        x], axis=1).reshape(batch * lp, D_MODEL)
    pad_col = jnp.zeros((lp, LANE), bf16).at[:PAD, 0].set(-NEG)
    w_in_b, w_out_b, w_up_b, w_down_b = (w.astype(bf16) for w in (w_in, w_out, w_up, w_down))
    for d in range(depth):
        w_f = jnp.pad(w_in[d, :, W_QKV:], ((0, 0), (0, LANE - N_HEADS))).astype(bf16)
        bias_f = jnp.pad(b_f[d], (0, LANE - N_HEADS)).reshape(1, LANE)
        qkv, lf = _inproj(h, g_mix_pre[d].reshape(1, D_MODEL), w_in_b, d, w_f, bias_f)
        c = _cumsum(lf, batch, lp)
        o_sb = _sb_attention(qkv, pad_col, g_sb[d].reshape(N_HEADS, 1, HEAD_DIM), batch, lp)
        o_fx = _fox_attention(qkv, c, g_fox[d].reshape(N_HEADS, 1, HEAD_DIM), batch, lp)
        h = _outproj(o_sb, o_fx, w_out_b, d, h, g_mix_post[d].reshape(1, D_MODEL), lp)
        a = _upconv(h, g_ffn_pre[d].reshape(1, D_MODEL), w_up_b, d, conv_w[d], conv_b[d].reshape(1, 2 * D_FF))
        g_post = g_ffn_post[d].reshape(1, D_MODEL)
        if d + 1 < depth:
            h = _downproj(a, w_down_b, d, h, g_post, lp)
        else:
            h = _downproj_tokens(a, w_down_b, d, h, g_post, batch, lp, seq)
    return h.reshape(batch, seq, D_MODEL)
```

```python
import math
from functools import partial

import jax
import jax.numpy as jnp
from jax import lax
from jax.experimental import pallas as pl
from jax.experimental.pallas import tpu as pltpu

D_MODEL = 2048
N_META = 16
HEAD_DIM = 128
N_HEADS = 8
W_GRP = N_HEADS * HEAD_DIM
W_QKV = 6 * W_GRP
D_FF = 11 * D_MODEL // 4
EPS = 1e-6
LOG2E = math.log2(math.e)
Q_SCALE = HEAD_DIM ** -0.5 * LOG2E

LANE = 128
SUBLANE = 8
BLK = 128
TQ = 256
HB = 4
GW = HB * HEAD_DIM
PAD = BLK - N_META
NEG = -1e30
VMEM_LIMIT = 58 * 1024 * 1024

f32 = jnp.float32
bf16 = jnp.bfloat16


def _rms(x, g):
    return x * lax.rsqrt(jnp.mean(x * x, axis=-1, keepdims=True) + EPS) * g


def _softplus(z):
    return jnp.maximum(z, 0.0) + jnp.log(1.0 + jnp.exp(-jnp.abs(z)))


def _softplus2(z):
    neg_abs = lax.bitcast_convert_type(
        lax.bitcast_convert_type(z, jnp.uint32) | jnp.uint32(0x80000000), f32)
    return jnp.maximum(z, 0.0) + jnp.log(1.0 + jnp.exp2(neg_abs)) * LOG2E


def _split3(x):
    h1 = x.astype(bf16)
    r = x - h1.astype(f32)
    h2 = r.astype(bf16)
    h3 = (r - h2.astype(f32)).astype(bf16)
    return h1, h2, h3


def _inproj_kernel(h_ref, g_ref, w_ref, wf_ref, bf_ref, qkv_ref, lf_ref, u_scr):
    j = pl.program_id(1)

    @pl.when(j == 0)
    def _():
        u = _rms(h_ref[...], g_ref[...]).astype(bf16)
        u_scr[...] = u
        f = jnp.dot(u, wf_ref[...], preferred_element_type=f32) + bf_ref[...]
        lf_ref[...] = -_softplus(-f)

    acc = jnp.dot(u_scr[...], w_ref[...], preferred_element_type=f32)
    is_q = jnp.logical_or(j == 0, j == 3)
    out = (acc * jnp.where(is_q, Q_SCALE, 1.0)).astype(bf16)
    for g in range(qkv_ref.shape[0]):
        qkv_ref[g] = out[:, g * GW:(g + 1) * GW]


def _inproj(h, g, w_in, layer, w_f, b_f, tm=768):
    rows = h.shape[0]
    tn = W_GRP
    return pl.pallas_call(
        _inproj_kernel,
        grid=(rows // tm, W_QKV // tn),
        in_specs=[
            pl.BlockSpec((tm, D_MODEL), lambda i, j: (i, 0)),
            pl.BlockSpec((1, D_MODEL), lambda i, j: (0, 0)),
            pl.BlockSpec((None, D_MODEL, tn), lambda i, j: (layer, 0, j)),
            pl.BlockSpec((D_MODEL, LANE), lambda i, j: (0, 0)),
            pl.BlockSpec((1, LANE), lambda i, j: (0, 0)),
        ],
        out_specs=[
            pl.BlockSpec((tn // GW, tm, GW), lambda i, j: (j, i, 0)),
            pl.BlockSpec((tm, LANE), lambda i, j: (i, 0)),
        ],
        out_shape=[
            jax.ShapeDtypeStruct((W_QKV // GW, rows, GW), bf16),
            jax.ShapeDtypeStruct((rows, LANE), f32),
        ],
        scratch_shapes=[pltpu.VMEM((tm, D_MODEL), bf16)],
        compiler_params=pltpu.CompilerParams(
            dimension_semantics=("parallel", "arbitrary"), vmem_limit_bytes=VMEM_LIMIT),
        name="inproj",
    )(h, g, w_in, w_f, b_f)


def _cumsum_kernel(lf_ref, c_ref, *, batch, lp):
    t = lax.broadcasted_iota(jnp.int32, (BLK, BLK), 0)
    s = lax.broadcasted_iota(jnp.int32, (BLK, BLK), 1)
    tri = (s <= t).astype(bf16)
    places = [jnp.logical_and(s == 3 * t + j, t < N_HEADS).astype(bf16) for j in range(3)]

    def body(i, carries):
        new = []
        for b, carry in enumerate(carries):
            off = pl.multiple_of(b * lp + i * BLK, BLK)
            tot = jnp.broadcast_to(carry, (BLK, LANE))
            for piece in _split3(lf_ref[pl.ds(off, BLK), :]):
                tot = tot + jnp.dot(tri, piece, preferred_element_type=f32)
            c2 = jnp.where(t + i * BLK < PAD, -NEG, tot * LOG2E)
            out = jnp.zeros((BLK, LANE), f32)
            for piece, place in zip(_split3(c2), places):
                out = out + jnp.dot(piece, place, preferred_element_type=f32)
            c_ref[pl.ds(off, BLK), :] = out.astype(bf16)
            new.append(tot[BLK - 1:BLK, :])
        return tuple(new)

    lax.fori_loop(0, lp // BLK, body, (jnp.zeros((1, LANE), f32),) * batch)


def _cumsum(lf, batch, lp):
    rows = batch * lp
    return pl.pallas_call(
        partial(_cumsum_kernel, batch=batch, lp=lp),
        grid=(1,),
        in_specs=[pl.BlockSpec((rows, LANE), lambda i: (0, 0))],
        out_specs=pl.BlockSpec((rows, LANE), lambda i: (0, 0)),
        out_shape=jax.ShapeDtypeStruct((rows, LANE), bf16),
        compiler_params=pltpu.CompilerParams(
            dimension_semantics=("arbitrary",), vmem_limit_bytes=VMEM_LIMIT),
        name="cumsum",
    )(lf)


def _later():
    s = lax.broadcasted_iota(jnp.int32, (BLK, BLK), 0)
    j = lax.broadcasted_iota(jnp.int32, (BLK, BLK), 1)
    return (j > s).astype(bf16)


def _key_query_iota(tk, tq):
    return (lax.broadcasted_iota(jnp.int32, (tk, tq), 0), lax.broadcasted_iota(jnp.int32, (tk, tq), 1))


def _head_cols():
    return [slice(hh * HEAD_DIM, (hh + 1) * HEAD_DIM) for hh in range(HB)]


def _store_vt(v_ref, vt_scr, nblk):
    def body(c, _):
        off = pl.multiple_of(c * BLK, BLK)
        for hh, cs in enumerate(_head_cols()):
            vt_scr[hh, c] = v_ref[0, pl.ds(off, BLK), cs].astype(f32).T.astype(bf16)
        return 0
    lax.fori_loop(0, nblk, body, 0)


def _vt_chunk(vt_scr, hh, blk0, tk):
    parts = [vt_scr[hh, blk0 + s] for s in range(tk // BLK)]
    return jnp.concatenate(parts, axis=1) if len(parts) > 1 else parts[0]


def _store_minus_rows(qt_scr, hh, first_row, n_rows):
    r = lax.broadcasted_iota(jnp.int32, (LANE, TQ), 0)
    hit = jnp.logical_and(r >= first_row, r < first_row + n_rows)
    qt_scr[hh, HEAD_DIM:, :] = jnp.where(hit, -1.0, 0.0).astype(bf16)


def _store_qt(q_ref, qt_scr, t0, tq):
    for hh, cs in enumerate(_head_cols()):
        qt_scr[hh, 0:HEAD_DIM, 0:tq] = q_ref[0, pl.ds(t0, tq), cs].astype(f32).T.astype(bf16)


def _scores(k_ref, x_ref, qt_scr, row0, tk, tq):
    extra = x_ref[pl.ds(row0, tk), :]
    return [jnp.dot(jnp.concatenate([k_ref[0, pl.ds(row0, tk), cs], extra], axis=1), qt_scr[hh, :, 0:tq],
                    preferred_element_type=f32)
            for hh, cs in enumerate(_head_cols())]


def _finish_head(acc_t, g_row):
    inv = lax.rsqrt(jnp.mean(acc_t * acc_t, axis=0, keepdims=True) + EPS)
    return ((acc_t * inv).T * g_row).astype(bf16)


def _attn_tile(i, tq, chunk, init, fold, finish, q_ref, k_ref, x_ref, o_ref, vt_scr, qt_scr, z_scr, acc_scr):
    nsub = TQ // BLK
    t0 = pl.multiple_of(i * TQ, TQ)
    _store_qt(q_ref, qt_scr, t0, tq)
    zs = _scores(k_ref, x_ref, qt_scr, t0, tq, tq)
    nxt = _scores(k_ref, x_ref, qt_scr, pl.multiple_of(jnp.maximum(t0 - TQ, 0), TQ), TQ, tq)
    state, _, outs = chunk(zs, [_vt_chunk(vt_scr, hh, i * nsub, tq) for hh in range(HB)], init(tq), True)
    for hh in range(HB):
        acc_scr[hh, :, 0:tq] = outs[hh]
        z_scr[hh, :, 0:tq] = nxt[hh]

    def body(n, state):
        c = i - 1 - n
        zs = [z_scr[hh, :, 0:tq] for hh in range(HB)]
        nxt = _scores(k_ref, x_ref, qt_scr, pl.multiple_of(jnp.maximum(c - 1, 0) * TQ, TQ), TQ, tq)
        state, scales, outs = chunk(zs, [_vt_chunk(vt_scr, hh, c * nsub, TQ) for hh in range(HB)], state, False)
        for hh in range(HB):
            acc_scr[hh, :, 0:tq] = fold(acc_scr[hh, :, 0:tq], scales[hh], outs[hh])
            z_scr[hh, :, 0:tq] = nxt[hh]
        return state

    state = lax.fori_loop(0, i, body, state)
    for hh, cs in enumerate(_head_cols()):
        o_ref[0, pl.ds(t0, tq), cs] = finish(acc_scr[hh, :, 0:tq], state, hh)


def _attn_kernel_body(tile, v_ref, vt_scr, nblk, nq, tq_last, first_step):
    i = pl.program_id(2)

    @pl.when(i == 0)
    def _():
        _store_vt(v_ref, vt_scr, nblk)
        first_step()

    @pl.when(i < nq - 1)
    def _():
        tile(i, TQ)

    @pl.when(i == nq - 1)
    def _():
        tile(i, tq_last)


def _sb_chunk(zs, vts, runs, diag, later):
    tk, tq = zs[0].shape
    nsub = tk // BLK
    if diag:
        kr, qc = _key_query_iota(tk, tq)
        valid = kr < qc
    ws, his, firsts = [], [], []
    for z in zs:
        sp = _softplus2(z)
        ws.append(z - sp)
        spm = jnp.where(valid, sp, 0.0) if diag else sp
        hi = spm.astype(bf16)
        his.append([hi[s * BLK:(s + 1) * BLK] for s in range(nsub)])
        firsts.append([spm[s * BLK:s * BLK + 1] for s in range(nsub)])
    withins = [[jnp.dot(later, blk, preferred_element_type=f32) for blk in hi] for hi in his]
    new_runs, outs = [], []
    for w, within, first, run, vt in zip(ws, withins, firsts, runs, vts):
        totals = [None] * nsub
        for s in reversed(range(nsub)):
            totals[s] = within[s] + run
            run = run + within[s][0:1] + first[s]
        total = jnp.concatenate(totals, axis=0) if nsub > 1 else totals[0]
        a = jnp.exp2(w - total)
        if diag:
            a = jnp.where(valid, a, 0.0)
        outs.append(jnp.dot(vt, a.astype(bf16), preferred_element_type=f32))
        new_runs.append(run)
    return new_runs, [None] * len(zs), outs


def _sb_kernel(q_ref, k_ref, v_ref, x_ref, g_ref, o_ref, vt_scr, qt_scr, z_scr, acc_scr, *, nblk, nq, tq_last):
    later = _later()
    tile = partial(
        _attn_tile,
        chunk=lambda zs, vts, runs, diag: _sb_chunk(zs, vts, runs, diag, later),
        init=lambda tq: [jnp.zeros((1, tq), f32)] * HB,
        fold=lambda acc, scale, out: acc + out,
        finish=lambda acc, runs, hh: _finish_head(acc, g_ref[hh]),
        q_ref=q_ref, k_ref=k_ref, x_ref=x_ref, o_ref=o_ref, vt_scr=vt_scr, qt_scr=qt_scr, z_scr=z_scr,
        acc_scr=acc_scr)

    def first_step():
        for hh in range(HB):
            _store_minus_rows(qt_scr, hh, 0, 1)

    _attn_kernel_body(tile, v_ref, vt_scr, nblk, nq, tq_last, first_step)


def _attn_call(kernel_fn, name, qkv, extra, g, batch, lp, first_part):
    ng = N_HEADS // HB
    nblk = lp // BLK
    nq = pl.cdiv(lp, TQ)
    part = lambda p: pl.BlockSpec((1, lp, GW), lambda b, g, i: ((first_part + p) * ng + g, b, 0))
    return pl.pallas_call(
        partial(kernel_fn, nblk=nblk, nq=nq, tq_last=lp - (nq - 1) * TQ),
        grid=(batch, ng, nq),
        in_specs=[part(0), part(1), part(2), extra[1],
                  pl.BlockSpec((HB, 1, HEAD_DIM), lambda b, g, i: (g, 0, 0))],
        out_specs=pl.BlockSpec((1, lp, GW), lambda b, g, i: (g, b, 0)),
        out_shape=jax.ShapeDtypeStruct((ng, batch * lp, GW), bf16),
        scratch_shapes=[
            pltpu.VMEM((HB, nblk, HEAD_DIM, BLK), bf16),
            pltpu.VMEM((HB, HEAD_DIM + LANE, TQ), bf16),
            pltpu.VMEM((HB, TQ, TQ), f32),
            pltpu.VMEM((HB, HEAD_DIM, TQ), f32),
        ],
        compiler_params=pltpu.CompilerParams(
            dimension_semantics=("parallel", "parallel", "arbitrary"), vmem_limit_bytes=VMEM_LIMIT),
        name=name,
    )(qkv, qkv, qkv, extra[0], g)


def _sb_attention(qkv, pad_col, g, batch, lp):
    spec = pl.BlockSpec((lp, LANE), lambda b, g, i: (0, 0))
    return _attn_call(_sb_kernel, "sb_attention", qkv, (pad_col, spec), g, batch, lp, 0)


def _fox_chunk(zs, vts, state, diag):
    tk, tq = zs[0].shape
    nh = len(zs)
    ms, ls = state[:nh], state[nh:]
    if diag:
        kr, qc = _key_query_iota(tk, tq)
        valid = kr <= qc
        zs = [jnp.where(valid, z, NEG) for z in zs]
    new_ms = [jnp.maximum(m, jnp.max(z, axis=0, keepdims=True)) for z, m in zip(zs, ms)]
    alphas = [jnp.exp2(m - m_new) for m, m_new in zip(ms, new_ms)]
    new_ls, outs = [], []
    for z, m_new, alpha, l, vt in zip(zs, new_ms, alphas, ls, vts):
        p = jnp.exp2(z - m_new)
        if diag:
            p = jnp.where(valid, p, 0.0)
        new_ls.append(alpha * l + jnp.sum(p, axis=0, keepdims=True))
        outs.append(jnp.dot(vt, p.astype(bf16), preferred_element_type=f32))
    return new_ms + new_ls, alphas, outs


def _fox_kernel(q_ref, k_ref, v_ref, x_ref, g_ref, o_ref, vt_scr, qt_scr, z_scr, acc_scr, *, nblk, nq, tq_last):
    head0 = pl.program_id(1) * HB

    def finish(acc, state, hh):
        l = state[HB + hh]
        return _finish_head(acc / jnp.where(l > 0.0, l, 1.0), g_ref[hh])

    tile = partial(
        _attn_tile,
        chunk=_fox_chunk,
        init=lambda tq: [jnp.full((1, tq), NEG, f32)] * HB + [jnp.zeros((1, tq), f32)] * HB,
        fold=lambda acc, scale, out: scale * acc + out,
        finish=finish,
        q_ref=q_ref, k_ref=k_ref, x_ref=x_ref, o_ref=o_ref, vt_scr=vt_scr, qt_scr=qt_scr, z_scr=z_scr,
        acc_scr=acc_scr)

    def first_step():
        for hh in range(HB):
            _store_minus_rows(qt_scr, hh, 3 * (head0 + hh), 3)

    _attn_kernel_body(tile, v_ref, vt_scr, nblk, nq, tq_last, first_step)


def _fox_attention(qkv, c, g, batch, lp):
    spec = pl.BlockSpec((lp, LANE), lambda b, g, i: (b, 0))
    return _attn_call(_fox_kernel, "fox_attention", qkv, (c, spec), g, batch, lp, 3)


def _row_is_token(i, tm, lp, rows):
    r = i * tm + lax.broadcasted_iota(jnp.int32, (tm, 1), 0)
    is_pad = jnp.zeros((tm, 1), jnp.bool_)
    for start in range(0, rows, lp):
        is_pad = jnp.logical_or(is_pad, jnp.logical_and(r >= start, r < start + PAD))
    return jnp.logical_not(is_pad)


def _outproj_kernel(osb_ref, ofx_ref, wa_ref, wb_ref, h_ref, g_ref, out_ref, *, tm, lp, rows):
    mix = jnp.zeros((tm, D_MODEL), f32)
    for o_ref, w_ref in ((osb_ref, wa_ref), (ofx_ref, wb_ref)):
        for grp in range(o_ref.shape[0]):
            mix = mix + jnp.dot(o_ref[grp], w_ref[grp * GW:(grp + 1) * GW, :], preferred_element_type=f32)
    hn = h_ref[...] + _rms(mix, g_ref[...])
    out_ref[...] = jnp.where(_row_is_token(pl.program_id(0), tm, lp, rows), hn, 0.0)


def _outproj(o_sb, o_fx, w_o, layer, h, g, lp, tm=512):
    rows = h.shape[0]
    ng = o_sb.shape[0]
    return pl.pallas_call(
        partial(_outproj_kernel, tm=tm, lp=lp, rows=rows),
        grid=(rows // tm,),
        in_specs=[
            pl.BlockSpec((ng, tm, GW), lambda i: (0, i, 0)),
            pl.BlockSpec((ng, tm, GW), lambda i: (0, i, 0)),
            pl.BlockSpec((None, W_GRP, D_MODEL), lambda i: (layer, 0, 0)),
            pl.BlockSpec((None, W_GRP, D_MODEL), lambda i: (layer, 1, 0)),
            pl.BlockSpec((tm, D_MODEL), lambda i: (i, 0)),
            pl.BlockSpec((1, D_MODEL), lambda i: (0, 0)),
        ],
        out_specs=pl.BlockSpec((tm, D_MODEL), lambda i: (i, 0)),
        out_shape=jax.ShapeDtypeStruct((rows, D_MODEL), f32),
        compiler_params=pltpu.CompilerParams(
            dimension_semantics=("parallel",), vmem_limit_bytes=VMEM_LIMIT),
        name="outproj",
    )(o_sb, o_fx, w_o, w_o, h, g)


def _upconv_kernel(h_ref, g_ref, wg_ref, wu_ref, cwg_ref, cwu_ref, cbg_ref, cbu_ref, out_ref,
                   u_scr, halo_scr, a_scr, *, tm):
    i = pl.program_id(0)
    j = pl.program_id(1)

    @pl.when(j == 0)
    def _():
        u_scr[...] = _rms(h_ref[...], g_ref[...]).astype(bf16)

    @pl.when(i == 0)
    def _():
        halo_scr[j] = jnp.zeros(halo_scr.shape[1:], f32)

    u = u_scr[...]

    def conv(w_ref, cw_ref, cb_ref, slot):
        scr = a_scr.at[slot]
        a = jnp.dot(u, w_ref[...], preferred_element_type=f32)
        scr[pl.ds(0, SUBLANE), :] = halo_scr[j, slot]
        scr[pl.ds(SUBLANE, tm), :] = a
        halo_scr[j, slot] = a[tm - SUBLANE:, :]
        cw = cw_ref[...]
        return (cw[0:1] * scr[pl.ds(SUBLANE - 2, tm), :] + cw[1:2] * scr[pl.ds(SUBLANE - 1, tm), :]
                + cw[2:3] * a + cb_ref[...])

    gate = conv(wg_ref, cwg_ref, cbg_ref, 0)
    up = conv(wu_ref, cwu_ref, cbu_ref, 1)
    out_ref[...] = (gate * (1.0 / (1.0 + jnp.exp(-gate))) * up).astype(bf16)


def _upconv(h, g, w_up, layer, conv_w, conv_b, tm=768, tn=512):
    rows = h.shape[0]
    nn = D_FF // tn
    return pl.pallas_call(
        partial(_upconv_kernel, tm=tm),
        grid=(rows // tm, nn),
        in_specs=[
            pl.BlockSpec((tm, D_MODEL), lambda i, j: (i, 0)),
            pl.BlockSpec((1, D_MODEL), lambda i, j: (0, 0)),
            pl.BlockSpec((None, D_MODEL, tn), lambda i, j: (layer, 0, j)),
            pl.BlockSpec((None, D_MODEL, tn), lambda i, j: (layer, 0, nn + j)),
            pl.BlockSpec((3, tn), lambda i, j: (0, j)),
            pl.BlockSpec((3, tn), lambda i, j: (0, nn + j)),
            pl.BlockSpec((1, tn), lambda i, j: (0, j)),
            pl.BlockSpec((1, tn), lambda i, j: (0, nn + j)),
        ],
        out_specs=pl.BlockSpec((tm, tn), lambda i, j: (i, j)),
        out_shape=jax.ShapeDtypeStruct((rows, D_FF), bf16),
        scratch_shapes=[
            pltpu.VMEM((tm, D_MODEL), bf16),
            pltpu.VMEM((nn, 2, SUBLANE, tn), f32),
            pltpu.VMEM((2, tm + SUBLANE, tn), f32),
        ],
        compiler_params=pltpu.CompilerParams(
            dimension_semantics=("arbitrary", "arbitrary"), vmem_limit_bytes=VMEM_LIMIT),
        name="upconv",
    )(h, g, w_up, w_up, conv_w, conv_w, conv_b, conv_b)


def _downproj_kernel(*refs, nparts, token_rows):
    a_refs, w_ref, h_refs = refs[:nparts], refs[nparts], refs[nparts + 1:2 * nparts + 1]
    g_ref, out_ref = refs[-2:]
    rows_of = lambda parts: parts[0][...] if nparts == 1 else jnp.concatenate([p[...] for p in parts], axis=0)
    ff = jnp.dot(rows_of(a_refs), w_ref[...], preferred_element_type=f32)
    hn = rows_of(h_refs) + _rms(ff, g_ref[...])
    out_ref[...] = hn if token_rows is None else jnp.where(token_rows(pl.program_id(0)), hn, 0.0)


def _downproj_call(a_specs, h_specs, out_spec, out_rows, steps, token_rows, a, w_down, layer, h, g):
    nparts = len(a_specs)
    w_spec = pl.BlockSpec((None, D_FF, D_MODEL), lambda i: (layer, 0, 0), pipeline_mode=pl.Buffered(1))
    return pl.pallas_call(
        partial(_downproj_kernel, nparts=nparts, token_rows=token_rows),
        grid=(steps,),
        in_specs=a_specs + [w_spec] + h_specs + [pl.BlockSpec((1, D_MODEL), lambda i: (0, 0))],
        out_specs=out_spec,
        out_shape=jax.ShapeDtypeStruct((out_rows, D_MODEL), f32),
        compiler_params=pltpu.CompilerParams(
            dimension_semantics=("parallel",), vmem_limit_bytes=VMEM_LIMIT),
        name="downproj",
    )(*([a] * nparts), w_down, *([h] * nparts), g)


def _downproj(a, w_down, layer, h, g, lp, tm=256):
    rows = h.shape[0]
    return _downproj_call(
        [pl.BlockSpec((tm, D_FF), lambda i: (i, 0))], [pl.BlockSpec((tm, D_MODEL), lambda i: (i, 0))],
        pl.BlockSpec((tm, D_MODEL), lambda i: (i, 0)), rows, rows // tm,
        partial(_row_is_token, tm=tm, lp=lp, rows=rows), a, w_down, layer, h, g)


def _downproj_tokens(a, w_down, layer, h, g, batch, lp, seq, tm=256):
    nparts = tm // BLK
    per_batch = seq // tm
    first = lambda i, p: (i // per_batch) * (lp // BLK) + 1 + (i % per_batch) * nparts + p
    return _downproj_call(
        [pl.BlockSpec((BLK, D_FF), lambda i, p=p: (first(i, p), 0)) for p in range(nparts)],
        [pl.BlockSpec((BLK, D_MODEL), lambda i, p=p: (first(i, p), 0)) for p in range(nparts)],
        pl.BlockSpec((tm, D_MODEL), lambda i: (i, 0)), batch * seq, batch * per_batch, None, a, w_down, layer,
        h, g)


def kernel(x, meta, g_mix_pre, w_in, b_f, g_sb, g_fox, w_out, g_mix_post, g_ffn_pre, w_up, conv_w, conv_b,
           w_down, g_ffn_post):
    batch, seq, _ = x.shape
    depth = w_in.shape[0]
    lp = PAD + N_META + seq
    h = jnp.concatenate([
        jnp.zeros((batch, PAD, D_MODEL), x.dtype),
        jnp.broadcast_to(meta[None].astype(x.dtype), (batch, N_META, D_MODEL)),
        x], axis=1).reshape(batch * lp, D_MODEL)
    pad_col = jnp.zeros((lp, LANE), bf16).at[:PAD, 0].set(-NEG)
    w_in_b, w_out_b, w_up_b, w_down_b = (w.astype(bf16) for w in (w_in, w_out, w_up, w_down))
    for d in range(depth):
        w_f = jnp.pad(w_in[d, :, W_QKV:], ((0, 0), (0, LANE - N_HEADS))).astype(bf16)
        bias_f = jnp.pad(b_f[d], (0, LANE - N_HEADS)).reshape(1, LANE)
        qkv, lf = _inproj(h, g_mix_pre[d].reshape(1, D_MODEL), w_in_b, d, w_f, bias_f)
        c = _cumsum(lf, batch, lp)
        o_sb = _sb_attention(qkv, pad_col, g_sb[d].reshape(N_HEADS, 1, HEAD_DIM), batch, lp)
        o_fx = _fox_attention(qkv, c, g_fox[d].reshape(N_HEADS, 1, HEAD_DIM), batch, lp)
        h = _outproj(o_sb, o_fx, w_out_b, d, h, g_mix_post[d].reshape(1, D_MODEL), lp)
        a = _upconv(h, g_ffn_pre[d].reshape(1, D_MODEL), w_up_b, d, conv_w[d], conv_b[d].reshape(1, 2 * D_FF))
        g_post = g_ffn_post[d].reshape(1, D_MODEL)
        if d + 1 < depth:
            h = _downproj(a, w_down_b, d, h, g_post, lp)
        else:
            h = _downproj_tokens(a, w_down_b, d, h, g_post, batch, lp, seq)
    return h.reshape(batch, seq, D_MODEL)
```
